```python
import jax, jax.numpy as jnp
from jax import lax
import numpy as np

D_MODEL = 1024
BATCH = 16
SEQ = 4096
DEPTH = 2

GRID_W = 64
CTX_LEN = 256
NA_HEAD_DIM = 64
NA_HEADS = (D_MODEL // 2) // NA_HEAD_DIM
NA_WIDTH = NA_HEADS * NA_HEAD_DIM
NA_WIN_H = 8
NA_WIN_W = 16
NA_QBLK_W = 16
NA_KBLK_W = 32
RET_HEAD_DIM = 128
RET_HEADS = (D_MODEL // 2) // RET_HEAD_DIM
RET_WIDTH = RET_HEADS * RET_HEAD_DIM
RET_CHUNK = 128
MIX_WIDTH = NA_WIDTH + RET_WIDTH
IN_COLS = 3 * NA_WIDTH + 5 * RET_WIDTH
ROPE_BASE = 10000.0
N_EXPERTS = 32
N_GROUPS = 8
EXPERTS_PER_GROUP = N_EXPERTS // N_GROUPS
TOP_K = 2
D_FF_EXPERT = D_MODEL // 2
MOE_BLOCK = 128
NORM_EPS = 1e-6
NEG_INF = -1e30

kernel_name = 'hybrid_na_retention_moe_dit'


def _rmsnorm(x, w):
    xf = x.astype(jnp.float32)
    y = xf * lax.rsqrt(jnp.mean(xf * xf, axis=-1, keepdims=True) + NORM_EPS)
    return (y * w.astype(jnp.float32)).astype(x.dtype)


def _modulate(x, w, shift, scale):
    return _rmsnorm(x, w) * (1.0 + scale) + shift


def _head_norm(o):
    mu = jnp.mean(o, axis=-1, keepdims=True)
    var = jnp.mean(jnp.square(o - mu), axis=-1, keepdims=True)
    return (o - mu) * lax.rsqrt(var + 1e-5)


def _axial_rope(x):
    B, L, H, d = x.shape
    half = d // 4
    t = jnp.arange(L)
    rows = (t // GRID_W).astype(jnp.float32)
    cols = (t % GRID_W).astype(jnp.float32)
    freqs = ROPE_BASE ** (-jnp.arange(half, dtype=jnp.float32) / half)
    ang = jnp.stack([rows[:, None] * freqs, cols[:, None] * freqs], axis=1)
    cos = jnp.cos(ang)[None, :, None]
    sin = jnp.sin(ang)[None, :, None]
    xr = x.astype(jnp.float32).reshape(B, L, H, 2, 2, half)
    x1, x2 = xr[..., 0, :], xr[..., 1, :]
    out = jnp.stack([x1 * cos - x2 * sin, x1 * sin + x2 * cos], axis=-2)
    return out.reshape(B, L, H, d).astype(x.dtype)


def _na_tables():
    n_cb = GRID_W // NA_QBLK_W
    qcol = np.arange(GRID_W).reshape(n_cb, NA_QBLK_W)
    kstart = np.clip(np.arange(n_cb) * NA_QBLK_W - NA_WIN_W // 2, 0, GRID_W - NA_KBLK_W)
    kcol = kstart[:, None] + np.arange(NA_KBLK_W)
    c0 = np.clip(qcol - NA_WIN_W // 2, 0, GRID_W - NA_WIN_W)
    valid = (kcol[:, None, :] >= c0[:, :, None]) & (kcol[:, None, :] < c0[:, :, None] + NA_WIN_W)
    dc_idx = np.clip(kcol[:, None, :] - qcol[:, :, None] + NA_WIN_W - 1, 0, 2 * NA_WIN_W - 2)
    return kcol, valid, dc_idx


def _neighborhood_attention(q, k, v, k_ctx, v_ctx, rpb):
    B, L, H, d = q.shape
    rows = L // GRID_W
    kh = min(NA_WIN_H, rows)
    kcol, valid, dc_idx = _na_tables()
    n_cb = GRID_W // NA_QBLK_W
    qg = q.reshape(B, rows, n_cb, NA_QBLK_W, H, d)
    kg = k.reshape(B, rows, GRID_W, H, d)
    vg = v.reshape(B, rows, GRID_W, H, d)
    scale = d ** -0.5
    mask = jnp.asarray(valid)[:, :, None, :]
    rpb_c = rpb[:, :, jnp.asarray(dc_idx)]
    n_lat = kh * NA_KBLK_W

    def row(r):
        r0 = jnp.clip(r - kh // 2, 0, rows - kh)
        kr = lax.dynamic_slice_in_dim(kg, r0, kh, axis=1)[:, :, kcol]
        vr = lax.dynamic_slice_in_dim(vg, r0, kh, axis=1)[:, :, kcol]
        qr = qg[:, r]
        s_lat = jnp.einsum('bjqhd,bkjnhd->bhjqkn', qr, kr).astype(jnp.float32) * scale
        dr_idx = r0 + jnp.arange(kh) - r + (NA_WIN_H - 1)
        bias = jnp.transpose(rpb_c[:, dr_idx], (0, 2, 3, 1, 4)).astype(jnp.float32)
        s_lat = jnp.where(mask, s_lat + bias, NEG_INF).reshape(B, H, n_cb, NA_QBLK_W, n_lat)
        s_ctx = jnp.einsum('bjqhd,bmhd->bhjqm', qr, k_ctx).astype(jnp.float32) * scale
        p = jax.nn.softmax(jnp.concatenate([s_lat, s_ctx], axis=-1), axis=-1).astype(v.dtype)
        p_lat = p[..., :n_lat].reshape(B, H, n_cb, NA_QBLK_W, kh, NA_KBLK_W)
        o = (jnp.einsum('bhjqkn,bkjnhd->bjqhd', p_lat, vr)
             + jnp.einsum('bhjqm,bmhd->bjqhd', p[..., n_lat:], v_ctx))
        return o.reshape(B, GRID_W, H, d)

    out = lax.map(row, jnp.arange(rows))
    return jnp.transpose(out, (1, 0, 2, 3, 4)).reshape(B, L, H, d)


def _ctx_attention(q, k, v):
    s = jnp.einsum('bqhd,bkhd->bhqk', q, k).astype(jnp.float32) * (q.shape[-1] ** -0.5)
    p = jax.nn.softmax(s, axis=-1).astype(v.dtype)
    return jnp.einsum('bhqk,bkhd->bqhd', p, v)


def _retention_final_state(k, v, log_g):
    L = k.shape[1]
    w = jnp.exp((L - 1.0 - jnp.arange(L, dtype=jnp.float32))[:, None] * log_g[None, :])
    return jnp.einsum('bmhd,bmhe->bhde', k * w[None, :, :, None], v)


def _retention_chunkwise(q, k, v, log_g, s0):
    B, L, H, d = q.shape
    C = RET_CHUNK
    n = L // C
    pos = jnp.arange(C, dtype=jnp.float32)
    diff = pos[:, None] - pos[None, :]
    intra = jnp.where(diff >= 0, jnp.exp(jnp.maximum(diff, 0.0)[None] * log_g[:, None, None]), 0.0)
    q_dec = jnp.exp((pos[:, None] + 1.0) * log_g[None, :])[None, :, :, None]
    k_dec = jnp.exp((C - 1.0 - pos)[:, None] * log_g[None, :])[None, :, :, None]
    c_dec = jnp.exp(C * log_g)[None, :, None, None]

    def chunks(a):
        return a.reshape(B, n, C, H, a.shape[-1]).swapaxes(0, 1)

    def step(s, blk):
        qb, kb, vb = blk
        scores = jnp.einsum('bihd,bjhd->bhij', qb, kb) * intra
        o = jnp.einsum('bhij,bjhe->bihe', scores, vb) + jnp.einsum('bihd,bhde->bihe', qb, s) * q_dec
        s = s * c_dec + jnp.einsum('bjhd,bjhe->bhde', kb * k_dec, vb)
        return s, o

    _, o = lax.scan(step, s0, (chunks(q), chunks(k), chunks(v)))
    return o.swapaxes(0, 1).reshape(B, L, H, v.shape[-1])


def _bidir_retention(q, k, v, g_f, g_b, log_g, s_f, s_b):
    B, L = q.shape[:2]
    o_f = _retention_chunkwise(q, k, v, log_g[0], s_f)
    o_b = jnp.flip(_retention_chunkwise(jnp.flip(q, 1), jnp.flip(k, 1), jnp.flip(v, 1), log_g[1], s_b), 1)
    o_f = _head_norm(o_f).reshape(B, L, RET_WIDTH).astype(g_f.dtype)
    o_b = _head_norm(o_b).reshape(B, L, RET_WIDTH).astype(g_b.dtype)
    return o_f * jax.nn.silu(g_f) + o_b * jax.nn.silu(g_b)


def _moe(h, router_w, router_bias, w_gate, w_up, w_down):
    T, D = h.shape
    s = jax.nn.sigmoid((h @ router_w).astype(jnp.float32))
    s_sel = (s + router_bias.astype(jnp.float32)).reshape(T, N_GROUPS, EXPERTS_PER_GROUP)
    grp = jnp.argmax(jnp.sum(lax.top_k(s_sel, 2)[0], axis=-1), axis=-1)
    in_grp = jnp.take_along_axis(s_sel, grp[:, None, None], axis=1)[:, 0]
    _, loc = lax.top_k(in_grp, TOP_K)
    eid = grp[:, None] * EXPERTS_PER_GROUP + loc
    gate = jnp.take_along_axis(s, eid, axis=1)
    gate = gate / jnp.sum(gate, axis=-1, keepdims=True)
    A = T * TOP_K
    e_flat = eid.reshape(A)
    tok_flat = jnp.repeat(jnp.arange(T, dtype=jnp.int32), TOP_K)
    order = jnp.argsort(e_flat)
    e_s, tok_s, w_s = e_flat[order], tok_flat[order], gate.reshape(A)[order]
    counts = jax.ops.segment_sum(jnp.ones((A,), jnp.int32), e_flat, num_segments=N_EXPERTS)
    start = jnp.cumsum(counts) - counts
    pcounts = (counts + MOE_BLOCK - 1) // MOE_BLOCK * MOE_BLOCK
    pend = jnp.cumsum(pcounts)
    pstart = pend - pcounts
    pos = pstart[e_s] + (jnp.arange(A, dtype=jnp.int32) - start[e_s])
    nb = (A + MOE_BLOCK - 1) // MOE_BLOCK + N_EXPERTS
    P = nb * MOE_BLOCK
    buf_tok = jnp.full((P,), T, jnp.int32).at[pos].set(tok_s)
    buf_w = jnp.zeros((P,), jnp.float32).at[pos].set(w_s)
    blk_start = jnp.arange(nb, dtype=jnp.int32) * MOE_BLOCK
    blk_exp = jnp.minimum(jnp.sum(blk_start[:, None] >= pend[None, :], axis=1), N_EXPERTS - 1)
    h_pad = jnp.concatenate([h, jnp.zeros((1, D), h.dtype)], axis=0)

    def run(args):
        tok_b, e_b = args
        xb = h_pad[tok_b]
        return (jax.nn.silu(xb @ w_gate[e_b]) * (xb @ w_up[e_b])) @ w_down[e_b]

    y = lax.map(run, (buf_tok.reshape(nb, MOE_BLOCK), blk_exp)).reshape(P, D)
    out = jnp.zeros((T + 1, D), h.dtype).at[buf_tok].add(y * buf_w[:, None].astype(y.dtype))
    return out[:T]


def _layer(x, ctx, c, c_ctx, ada_w, ada_b, norm_mix_w, norm_ffn_w, w_in, na_q_norm, na_k_norm,
           na_rpb, ret_decay, w_out, router_w, router_bias, w_gate, w_up, w_down, last):
    B, L, D = x.shape
    Lc = ctx.shape[1]
    mod = jax.nn.silu(c) @ ada_w + ada_b
    mod_c = jax.nn.silu(c_ctx) @ ada_w + ada_b
    sh_a, sc_a, g_a, sh_f, sc_f, g_f = jnp.split(mod[:, None, :], 6, axis=-1)
    sh_ac, sc_ac, g_ac, sh_fc, sc_fc, g_fc = jnp.split(mod_c, 6)
    splits = list(np.cumsum([NA_WIDTH] * 3 + [RET_WIDTH] * 4))

    def project(h):
        Bx, Lx = h.shape[:2]
        qa, ka, va, qb, kb, vb, gfw, gbw = jnp.split(h @ w_in, splits, axis=-1)
        na = [a.reshape(Bx, Lx, NA_HEADS, NA_HEAD_DIM) for a in (qa, ka, va)]
        rt = [a.reshape(Bx, Lx, RET_HEADS, RET_HEAD_DIM) for a in (qb, kb, vb)]
        na[0] = _rmsnorm(na[0], na_q_norm)
        na[1] = _rmsnorm(na[1], na_k_norm)
        return na, rt, gfw, gbw

    h = _modulate(x, norm_mix_w, sh_a, sc_a)
    hc = _modulate(ctx, norm_mix_w, sh_ac, sc_ac)
    (qa, ka, va), (qb, kb, vb), gfw, gbw = project(h)
    (qac, kac, vac), (qbc, kbc, vbc), gfc, gbc = project(hc)
    ret_scale = RET_HEAD_DIM ** -0.5
    qb = _axial_rope(qb).astype(jnp.float32)
    kb = _axial_rope(kb).astype(jnp.float32) * ret_scale
    vb = vb.astype(jnp.float32)
    qbc = qbc.astype(jnp.float32)
    kbc = kbc.astype(jnp.float32) * ret_scale
    vbc = vbc.astype(jnp.float32)
    log_g = -jnp.exp(ret_decay.astype(jnp.float32))
    s_f = _retention_final_state(kbc, vbc, log_g[0])
    s_b = _retention_final_state(jnp.flip(kbc, 1), jnp.flip(vbc, 1), log_g[1])

    na_lat = _neighborhood_attention(qa, ka, va, kac, vac, na_rpb).reshape(B, L, NA_WIDTH)
    ret_lat = _bidir_retention(qb, kb, vb, gfw, gbw, log_g, s_f, s_b)
    x = x + g_a * (jnp.concatenate([na_lat, ret_lat], axis=-1) @ w_out)
    if not last:
        zero = jnp.zeros_like(s_f)
        na_ctx = _ctx_attention(qac, kac, vac).reshape(B, Lc, NA_WIDTH)
        ret_ctx = _bidir_retention(qbc, kbc, vbc, gfc, gbc, log_g, zero, zero)
        ctx = ctx + g_ac * (jnp.concatenate([na_ctx, ret_ctx], axis=-1) @ w_out)

    tokens = _modulate(x, norm_ffn_w, sh_f, sc_f).reshape(B * L, D)
    if not last:
        tokens_c = _modulate(ctx, norm_ffn_w, sh_fc, sc_fc).reshape(B * Lc, D)
        tokens = jnp.concatenate([tokens, tokens_c], axis=0)
    y = _moe(tokens, router_w, router_bias, w_gate, w_up, w_down)
    x = x + g_f * y[:B * L].reshape(B, L, D)
    if not last:
        ctx = ctx + g_fc * y[B * L:].reshape(B, Lc, D)
    return x, ctx


def setup_inputs(seed: int = 0) -> dict:
    key = jax.random.key(seed)
    ks = jax.random.split(key, 20)
    f32 = jnp.float32
    nrm = lambda k, shape, s: jax.random.normal(k, shape, f32) * s
    base_decay = np.log(-np.log(1.0 - 2.0 ** (-5.0 - np.arange(RET_HEADS)))).astype(np.float32)
    return {
        'x': nrm(ks[0], (BATCH, SEQ, D_MODEL), 1.0),
        'c': nrm(ks[1], (BATCH, D_MODEL), 1.0),
        'ctx': nrm(ks[2], (BATCH, CTX_LEN, D_MODEL), 1.0),
        'c_ctx': nrm(ks[3], (D_MODEL,), 1.0),
        'ada_w': nrm(ks[4], (DEPTH, D_MODEL, 6 * D_MODEL), 0.5 * D_MODEL ** -0.5),
        'ada_b': nrm(ks[5], (DEPTH, 6 * D_MODEL), 0.02),
        'norm_mix_w': 1.0 + nrm(ks[6], (DEPTH, D_MODEL), 0.02),
        'norm_ffn_w': 1.0 + nrm(ks[7], (DEPTH, D_MODEL), 0.02),
        'w_in': nrm(ks[8], (DEPTH, D_MODEL, IN_COLS), D_MODEL ** -0.5),
        'na_q_norm': 1.0 + nrm(ks[9], (DEPTH, NA_HEAD_DIM), 0.02),
        'na_k_norm': 1.0 + nrm(ks[10], (DEPTH, NA_HEAD_DIM), 0.02),
        'na_rpb': nrm(ks[11], (DEPTH, NA_HEADS, 2 * NA_WIN_H - 1, 2 * NA_WIN_W - 1), 0.02),
        'ret_decay': jnp.asarray(base_decay)[None, None, :] + nrm(ks[12], (DEPTH, 2, RET_HEADS), 0.05),
        'w_out': nrm(ks[13], (DEPTH, MIX_WIDTH, D_MODEL), MIX_WIDTH ** -0.5),
        'router_w': nrm(ks[14], (D_MODEL, N_EXPERTS), D_MODEL ** -0.5),
        'router_bias': nrm(ks[15], (N_EXPERTS,), 0.01),
        'exp_w_gate': nrm(ks[16], (DEPTH, N_EXPERTS, D_MODEL, D_FF_EXPERT), D_MODEL ** -0.5),
        'exp_w_up': nrm(ks[17], (DEPTH, N_EXPERTS, D_MODEL, D_FF_EXPERT), D_MODEL ** -0.5),
        'exp_w_down': nrm(ks[18], (DEPTH, N_EXPERTS, D_FF_EXPERT, D_MODEL), D_FF_EXPERT ** -0.5),
    }


def reference(x, c, ctx, c_ctx, ada_w, ada_b, norm_mix_w, norm_ffn_w, w_in, na_q_norm, na_k_norm,
              na_rpb, ret_decay, w_out, router_w, router_bias, exp_w_gate, exp_w_up, exp_w_down):
    for l in range(DEPTH):
        x, ctx = _layer(x, ctx, c, c_ctx, ada_w[l], ada_b[l], norm_mix_w[l], norm_ffn_w[l], w_in[l],
                        na_q_norm[l], na_k_norm[l], na_rpb[l], ret_decay[l], w_out[l],
                        router_w, router_bias, exp_w_gate[l], exp_w_up[l], exp_w_down[l],
                        l == DEPTH - 1)
    return x
```

```python
import functools

import numpy as np
import jax
import jax.numpy as jnp
from jax import lax
from jax.experimental import pallas as pl
from jax.experimental.pallas import tpu as pltpu

F32 = jnp.float32
BF16 = jnp.bfloat16
I32 = jnp.int32
U32 = jnp.uint32

GRID_W = 64
NA_HEAD_DIM = 64
NA_HEADS = 8
NA_WIDTH = NA_HEADS * NA_HEAD_DIM
NA_WIN_H = 8
NA_WIN_W = 16
NA_Q_ROWS = 4
NA_K_ROWS = NA_Q_ROWS + NA_WIN_H
RET_HEAD_DIM = 128
RET_HEADS = 4
RET_WIDTH = RET_HEADS * RET_HEAD_DIM
RET_CHUNK = 128
ROPE_BASE = 10000.0
N_EXPERTS = 32
N_GROUPS = 8
EXPERTS_PER_GROUP = N_EXPERTS // N_GROUPS
NORM_EPS = 1e-6
HEAD_NORM_EPS = 1e-5
NEG_INF = -1e30
MOE_BLOCK_ROWS = 256
RANK_BITS = 20
TOKEN_TILE = 512
COMBINE_TILE = 256
MOD_ROWS = 24
V7X_VMEM_LIMIT_BYTES = 56 * 1024 * 1024

_NT = (((1,), (1,)), ((), ()))
_TN = (((0,), (0,)), ((), ()))


def _params(n_axes):
    return pltpu.CompilerParams(dimension_semantics=("arbitrary",) * n_axes,
                                vmem_limit_bytes=V7X_VMEM_LIMIT_BYTES)


def _silu(v):
    return v * jax.nn.sigmoid(v)


def _pack_bf16_pair(v):
    w = v.shape[1] // 2
    lo = lax.bitcast_convert_type(v[:, :w].astype(BF16).astype(F32), U32) >> 16
    hi = lax.bitcast_convert_type(v[:, w:].astype(BF16).astype(F32), U32) & jnp.uint32(0xFFFF0000)
    return hi | lo


def _unpack_bf16_pair(p):
    lo = lax.bitcast_convert_type(p << 16, F32)
    hi = lax.bitcast_convert_type(p & jnp.uint32(0xFFFF0000), F32)
    return lo, hi


def _ada_kernel(cc_ref, w_ref, b_ref, o_ref):
    a = _silu(cc_ref[...])
    o_ref[0] = jnp.dot(a, w_ref[0], preferred_element_type=F32, precision=lax.Precision.HIGHEST) + b_ref[0]


def _ada(cc, ada_w, ada_b):
    depth, d, n = ada_w.shape
    tn = 1536
    return pl.pallas_call(
        _ada_kernel,
        grid=(depth, n // tn),
        in_specs=[pl.BlockSpec((MOD_ROWS, d), lambda l, j: (0, 0)),
                  pl.BlockSpec((1, d, tn), lambda l, j: (l, 0, j)),
                  pl.BlockSpec((1, 1, tn), lambda l, j: (l, 0, j))],
        out_specs=pl.BlockSpec((1, MOD_ROWS, tn), lambda l, j: (l, 0, j)),
        out_shape=jax.ShapeDtypeStruct((depth, MOD_ROWS, n), F32),
        compiler_params=_params(2),
        name="ada_mod",
    )(cc, ada_w, ada_b.reshape(depth, 1, n))


def _proj_kernel(*refs, rope):
    if rope:
        (x_ref, sh_ref, sc_ref, nw_ref, w_ref, grp_ref, qn_ref, kn_ref, cos_ref, sin_ref,
         qa_ref, ka_ref, va_ref, qb_ref, kb_ref, vb_ref, gf_ref, gb_ref) = refs
    else:
        (x_ref, sh_ref, sc_ref, nw_ref, w_ref, grp_ref, qn_ref, kn_ref,
         qa_ref, ka_ref, va_ref, qb_ref, kb_ref, vb_ref, gf_ref, gb_ref) = refs
    x = x_ref[0]
    ms = jnp.mean(x * x, axis=-1, keepdims=True)
    h = x * lax.rsqrt(ms + NORM_EPS) * nw_ref[...]
    h = (h * (1.0 + sc_ref[0]) + sh_ref[0]).astype(BF16)

    def mm(g):
        return jnp.dot(h, w_ref[:, g * NA_WIDTH:(g + 1) * NA_WIDTH], preferred_element_type=F32)

    def head_rms(a, wn):
        ss = jnp.dot((a * a).astype(BF16), grp_ref[...], preferred_element_type=F32)
        return a * lax.rsqrt(ss + NORM_EPS) * wn

    def rope_fn(a):
        if not rope:
            return a
        lane = lax.broadcasted_iota(I32, (1, RET_HEAD_DIM), 1)
        first = (lane % 64) < 32
        cos = cos_ref[...]
        sin = sin_ref[...]
        parts = []
        for hh in range(RET_HEADS):
            ah = a[:, hh * RET_HEAD_DIM:(hh + 1) * RET_HEAD_DIM]
            rot = jnp.where(first, pltpu.roll(ah, RET_HEAD_DIM - 32, 1), pltpu.roll(ah, 32, 1))
            parts.append(ah * cos + rot * sin)
        return jnp.concatenate(parts, axis=1)

    qa_ref[0] = (head_rms(mm(0), qn_ref[...]) * (NA_HEAD_DIM ** -0.5)).astype(BF16)
    ka_ref[0] = head_rms(mm(1), kn_ref[...]).astype(BF16)
    va_ref[0] = mm(2).astype(BF16)
    qb_ref[0] = rope_fn(mm(3)).astype(BF16)
    kb_ref[0] = (rope_fn(mm(4)) * (RET_HEAD_DIM ** -0.5)).astype(BF16)
    vb_ref[0] = mm(5).astype(BF16)
    gf_ref[0] = _silu(mm(6)).astype(BF16)
    gb_ref[0] = _silu(mm(7)).astype(BF16)


def _proj(x, mod3, nw, w_in, grp, qn, kn, rope_tabs, *, ctx_row):
    b, lx, d = x.shape
    rope = ctx_row is None
    tm = min(TOKEN_TILE, lx)
    n_cols = w_in.shape[1]
    if rope:
        mrow = lambda i, bb: bb
    else:
        mrow = lambda i, bb: ctx_row
    in_specs = [
        pl.BlockSpec((1, tm, d), lambda i, bb: (bb, i, 0)),
        pl.BlockSpec((1, 1, d), lambda i, bb: (mrow(i, bb), 0, 0)),
        pl.BlockSpec((1, 1, d), lambda i, bb: (mrow(i, bb), 0, 1)),
        pl.BlockSpec((1, d), lambda i, bb: (0, 0)),
        pl.BlockSpec((d, n_cols), lambda i, bb: (0, 0)),
        pl.BlockSpec((NA_WIDTH, NA_WIDTH), lambda i, bb: (0, 0)),
        pl.BlockSpec((1, NA_WIDTH), lambda i, bb: (0, 0)),
        pl.BlockSpec((1, NA_WIDTH), lambda i, bb: (0, 0)),
    ]
    args = [x, mod3, mod3, nw, w_in, grp, qn, kn]
    if rope:
        in_specs += [pl.BlockSpec((tm, RET_HEAD_DIM), lambda i, bb: (i, 0))] * 2
        args += list(rope_tabs)
    out_spec = pl.BlockSpec((1, tm, NA_WIDTH), lambda i, bb: (bb, i, 0))
    out_shape = jax.ShapeDtypeStruct((b, lx, NA_WIDTH), BF16)
    return pl.pallas_call(
        functools.partial(_proj_kernel, rope=rope),
        grid=(lx // tm, b),
        in_specs=in_specs,
        out_specs=[out_spec] * 8,
        out_shape=[out_shape] * 8,
        compiler_params=_params(2),
        name="in_proj_lat" if rope else "in_proj_ctx",
    )(*args)


def _attn_kernel(*refs, latent, n_steps):
    if latent:
        q_ref, k_ref, v_ref, kc_ref, vc_ref, bias_ref, o_ref = refs
        i = pl.program_id(1)
        rows = n_steps * NA_Q_ROWS
        ws = jnp.clip(i * NA_Q_ROWS - NA_WIN_H // 2, 0, rows - NA_K_ROWS)
        kstart = pl.multiple_of(ws * GRID_W, GRID_W)
    else:
        q_ref, kc_ref, vc_ref, o_ref = refs
    lane = lax.broadcasted_iota(I32, (1, 2 * NA_HEAD_DIM), 1)
    lo = lane < NA_HEAD_DIM
    for j in range(NA_HEADS // 2):
        cs = slice(j * 2 * NA_HEAD_DIM, (j + 1) * 2 * NA_HEAD_DIM)
        qp = q_ref[0, :, cs]
        kcp = kc_ref[0, :, cs]
        vcp = vc_ref[0, :, cs]
        if latent:
            kp = k_ref[0, pl.ds(kstart, NA_K_ROWS * GRID_W), cs]
            vp = v_ref[0, pl.ds(kstart, NA_K_ROWS * GRID_W), cs]
        halves = []
        for half in range(2):
            qm = jnp.where(lo if half == 0 else jnp.logical_not(lo), qp, jnp.zeros_like(qp))
            s_ctx = lax.dot_general(qm, kcp, _NT, preferred_element_type=F32)
            m = jnp.max(s_ctx, axis=-1, keepdims=True)
            if latent:
                s_lat = lax.dot_general(qm, kp, _NT, preferred_element_type=F32) + bias_ref[0, 2 * j + half]
                m = jnp.maximum(m, jnp.max(s_lat, axis=-1, keepdims=True))
            p_ctx = jnp.exp(s_ctx - m)
            den = jnp.sum(p_ctx, axis=-1, keepdims=True)
            o = jnp.dot(p_ctx.astype(BF16), vcp, preferred_element_type=F32)
            if latent:
                p_lat = jnp.exp(s_lat - m)
                den = den + jnp.sum(p_lat, axis=-1, keepdims=True)
                o = o + jnp.dot(p_lat.astype(BF16), vp, preferred_element_type=F32)
            halves.append(o / den)
        o_ref[0, :, cs] = jnp.where(lo, halves[0], halves[1]).astype(BF16)


def _na_attention(qa, ka, va, kc, vc, bias):
    b, l, w = qa.shape
    lc = kc.shape[1]
    tq = NA_Q_ROWS * GRID_W
    tk = NA_K_ROWS * GRID_W
    n_steps = l // tq
    case = lambda i: jnp.where(i == 0, 0, jnp.where(i == n_steps - 1, 2, 1))
    return pl.pallas_call(
        functools.partial(_attn_kernel, latent=True, n_steps=n_steps),
        grid=(b, n_steps),
        in_specs=[pl.BlockSpec((1, tq, w), lambda bb, i: (bb, i, 0)),
                  pl.BlockSpec((1, l, w), lambda bb, i: (bb, 0, 0)),
                  pl.BlockSpec((1, l, w), lambda bb, i: (bb, 0, 0)),
                  pl.BlockSpec((1, lc, w), lambda bb, i: (bb, 0, 0)),
                  pl.BlockSpec((1, lc, w), lambda bb, i: (bb, 0, 0)),
                  pl.BlockSpec((1, NA_HEADS, tq, tk), lambda bb, i: (case(i), 0, 0, 0))],
        out_specs=pl.BlockSpec((1, tq, w), lambda bb, i: (bb, i, 0)),
        out_shape=jax.ShapeDtypeStruct((b, l, w), BF16),
        compiler_params=_params(2),
        name="na_attention",
    )(qa, ka, va, kc, vc, bias)


def _ctx_attention(qc, kc, vc):
    b, lc, w = qc.shape
    spec = pl.BlockSpec((1, lc, w), lambda bb: (bb, 0, 0))
    return pl.pallas_call(
        functools.partial(_attn_kernel, latent=False, n_steps=1),
        grid=(b,),
        in_specs=[spec, spec, spec],
        out_specs=spec,
        out_shape=jax.ShapeDtypeStruct((b, lc, w), BF16),
        compiler_params=_params(1),
        name="ctx_attention",
    )(qc, kc, vc)


def _na_bias_table(rpb):
    h = rpb.shape[0]
    qcol = np.arange(GRID_W)
    kcol = np.arange(GRID_W)
    c0 = np.clip(qcol - NA_WIN_W // 2, 0, GRID_W - NA_WIN_W)
    col_ok = (kcol[None, :] >= c0[:, None]) & (kcol[None, :] < c0[:, None] + NA_WIN_W)
    dc = np.clip(kcol[None, :] - qcol[:, None] + NA_WIN_W - 1, 0, 2 * NA_WIN_W - 2)
    by_dr = jnp.where(jnp.asarray(col_ok)[None, None], rpb[:, :, jnp.asarray(dc)], NEG_INF)
    masked = jnp.full((h, GRID_W, GRID_W), NEG_INF, F32)
    cases = []
    for lo_fn, dr_off in ((lambda qr: 0, NA_WIN_H - 1),
                          (lambda qr: qr, NA_WIN_H // 2 - 1),
                          (lambda qr: NA_K_ROWS - NA_WIN_H, -1)):
        rows = []
        for qr in range(NA_Q_ROWS):
            blocks = []
            for kr in range(NA_K_ROWS):
                ok = lo_fn(qr) <= kr < lo_fn(qr) + NA_WIN_H
                blocks.append(by_dr[:, kr - qr + dr_off] if ok else masked)
            rows.append(jnp.concatenate(blocks, axis=2))
        cases.append(jnp.concatenate(rows, axis=1))
    return jnp.stack(cases, axis=0)


def _ret_kernel(rd_ref, q_ref, k_ref, v_ref, gf_ref, gb_ref, s0_ref, o_ref, sfin_ref, acc_ref, *, n_chunks):
    c = RET_CHUNK
    hd = pl.program_id(1)
    log_g_all = -jnp.exp(rd_ref[...])
    head = lax.broadcasted_iota(I32, log_g_all.shape, 1)
    log_g = jnp.sum(jnp.where(head == hd, log_g_all, 0.0), axis=1, keepdims=True)
    lg_f = log_g[0:1, :]
    lg_b = log_g[1:2, :]
    diff = (lax.broadcasted_iota(I32, (c, c), 0) - lax.broadcasted_iota(I32, (c, c), 1)).astype(F32)
    pos = lax.broadcasted_iota(I32, (c, 1), 0).astype(F32)

    def scan(lg, causal, s0, emit):
        if causal:
            intra = jnp.where(diff >= 0, jnp.exp(jnp.maximum(diff, 0.0) * lg), 0.0)
            q_dec = jnp.exp((pos + 1.0) * lg)
            k_dec = jnp.exp((c - 1.0 - pos) * lg)
        else:
            intra = jnp.where(diff <= 0, jnp.exp(jnp.maximum(-diff, 0.0) * lg), 0.0)
            q_dec = jnp.exp((c - pos) * lg)
            k_dec = jnp.exp(pos * lg)
        c_dec = jnp.exp(c * lg)

        def step(t, s):
            n = t if causal else n_chunks - 1 - t
            off = pl.multiple_of(n * c, c)
            qb = q_ref[0, pl.ds(off, c), :]
            kb = k_ref[0, pl.ds(off, c), :]
            vb = v_ref[0, pl.ds(off, c), :]
            scores = lax.dot_general(qb, kb, _NT, preferred_element_type=F32) * intra
            o = (jnp.dot(scores.astype(BF16), vb, preferred_element_type=F32)
                 + jnp.dot(qb, s.astype(BF16), preferred_element_type=F32) * q_dec)
            kd = (kb.astype(F32) * k_dec).astype(BF16)
            s = s * c_dec + lax.dot_general(kd, vb, _TN, preferred_element_type=F32)
            mu = jnp.mean(o, axis=-1, keepdims=True)
            oc = o - mu
            var = jnp.mean(oc * oc, axis=-1, keepdims=True)
            emit(off, oc * lax.rsqrt(var + HEAD_NORM_EPS))
            return s

        return lax.fori_loop(0, n_chunks, step, s0)

    def emit_fwd(off, on):
        acc_ref[pl.ds(off, c), :] = on * gf_ref[0, pl.ds(off, c), :].astype(F32)

    def emit_bwd(off, on):
        o_ref[0, pl.ds(off, c), :] = (acc_ref[pl.ds(off, c), :]
                                      + on * gb_ref[0, pl.ds(off, c), :].astype(F32)).astype(BF16)

    sfin_ref[0, 0, 0] = scan(lg_f, True, s0_ref[0, 0, 0], emit_fwd)
    sfin_ref[0, 0, 1] = scan(lg_b, False, s0_ref[0, 0, 1], emit_bwd)


def _retention(ret_decay, q, k, v, gf, gb, s0):
    b, lx, w = q.shape
    hd = RET_HEAD_DIM
    seq = pl.BlockSpec((1, lx, hd), lambda bb, h: (bb, 0, h))
    st = pl.BlockSpec((1, 1, 2, hd, hd), lambda bb, h: (bb, h, 0, 0, 0))
    return pl.pallas_call(
        functools.partial(_ret_kernel, n_chunks=lx // RET_CHUNK),
        grid=(b, RET_HEADS),
        in_specs=[pl.BlockSpec((2, RET_HEADS), lambda bb, h: (0, 0)), seq, seq, seq, seq, seq, st],
        out_specs=[seq, st],
        out_shape=[jax.ShapeDtypeStruct((b, lx, w), BF16),
                   jax.ShapeDtypeStruct((b, RET_HEADS, 2, hd, hd), F32)],
        scratch_shapes=[pltpu.VMEM((lx, hd), F32)],
        compiler_params=_params(2),
        name="retention",
    )(ret_decay, q, k, v, gf, gb, s0)


def _mix_kernel(x_ref, na_ref, rt_ref, wo_ref, ga_ref, shf_ref, scf_ref, nw_ref, rwt_ref, rb_ref, tri_ref,
                cnt_in_ref, xo_ref, tok_ref, code_ref, gate_ref, cnt_out_ref, cnt_sc):
    @pl.when(pl.program_id(0) == 0)
    def _():
        cnt_sc[...] = cnt_in_ref[...]

    mix = (jnp.dot(na_ref[...], wo_ref[0:NA_WIDTH, :], preferred_element_type=F32)
           + jnp.dot(rt_ref[...], wo_ref[NA_WIDTH:, :], preferred_element_type=F32))
    x = x_ref[...] + ga_ref[0] * mix
    xo_ref[...] = x
    ms = jnp.mean(x * x, axis=-1, keepdims=True)
    t = x * lax.rsqrt(ms + NORM_EPS) * nw_ref[...]
    t = t * (1.0 + scf_ref[0]) + shf_ref[0]
    tok_ref[...] = _pack_bf16_pair(t)

    tm = x.shape[0]
    logits = lax.dot_general(rwt_ref[...], t.astype(BF16), _NT, preferred_element_type=F32)
    s = jax.nn.sigmoid(logits)
    s_sel = s + rb_ref[...]
    a = [s_sel[j * N_GROUPS:(j + 1) * N_GROUPS, :] for j in range(EXPERTS_PER_GROUP)]
    u = [s[j * N_GROUPS:(j + 1) * N_GROUPS, :] for j in range(EXPERTS_PER_GROUP)]
    hi01, lo01 = jnp.maximum(a[0], a[1]), jnp.minimum(a[0], a[1])
    hi23, lo23 = jnp.maximum(a[2], a[3]), jnp.minimum(a[2], a[3])
    gscore = jnp.maximum(hi01, hi23) + jnp.maximum(jnp.minimum(hi01, hi23), jnp.maximum(lo01, lo23))
    gidx = lax.broadcasted_iota(I32, (N_GROUPS, tm), 0)
    gmax = jnp.max(gscore, axis=0, keepdims=True)
    grp = jnp.min(jnp.where(gscore == gmax, gidx.astype(F32), float(N_GROUPS)), axis=0,
                  keepdims=True).astype(I32)
    sel = gidx == grp
    av = [jnp.sum(jnp.where(sel, a[j], 0.0), axis=0, keepdims=True) for j in range(EXPERTS_PER_GROUP)]
    uv = [jnp.sum(jnp.where(sel, u[j], 0.0), axis=0, keepdims=True) for j in range(EXPERTS_PER_GROUP)]
    best1, loc1 = av[0], jnp.zeros((1, tm), I32)
    for j in range(1, EXPERTS_PER_GROUP):
        take = av[j] > best1
        best1 = jnp.where(take, av[j], best1)
        loc1 = jnp.where(take, j, loc1)
    best2, loc2 = jnp.full((1, tm), -jnp.inf, F32), jnp.full((1, tm), -1, I32)
    for j in range(EXPERTS_PER_GROUP):
        take = jnp.logical_and(loc1 != j, jnp.logical_or(loc2 < 0, av[j] > best2))
        best2 = jnp.where(take, av[j], best2)
        loc2 = jnp.where(take, j, loc2)
    g1 = sum(jnp.where(loc1 == j, uv[j], 0.0) for j in range(EXPERTS_PER_GROUP))
    g2 = sum(jnp.where(loc2 == j, uv[j], 0.0) for j in range(EXPERTS_PER_GROUP))
    gate_ref[0:1, :] = g1 / (g1 + g2)
    gate_ref[1:2, :] = g2 / (g1 + g2)

    r1 = loc1 * N_GROUPS + grp
    r2 = loc2 * N_GROUPS + grp
    ridx = lax.broadcasted_iota(I32, (N_EXPERTS, tm), 0)
    hit1 = ridx == r1
    hit2 = ridx == r2
    onehot = jnp.logical_or(hit1, hit2).astype(F32)
    before = jnp.dot(onehot.astype(BF16), tri_ref[...], preferred_element_type=F32) + cnt_sc[...]
    rank1 = jnp.sum(jnp.where(hit1, before, 0.0), axis=0, keepdims=True).astype(I32)
    rank2 = jnp.sum(jnp.where(hit2, before, 0.0), axis=0, keepdims=True).astype(I32)
    code_ref[0:1, :] = (r1 << RANK_BITS) | rank1
    code_ref[1:2, :] = (r2 << RANK_BITS) | rank2
    cnt_sc[...] = cnt_sc[...] + jnp.sum(onehot, axis=1, keepdims=True)
    cnt_out_ref[...] = cnt_sc[...]


def _mix(xf, na, rt, w_out, mod3, nw, rwt, rb, tri, cnt_in, *, tiles_per_row, ctx_row):
    t, d = xf.shape
    tm = TOKEN_TILE
    if ctx_row is None:
        mrow = lambda j: j // tiles_per_row
    else:
        mrow = lambda j: ctx_row
    half = d // 2
    return pl.pallas_call(
        _mix_kernel,
        grid=(t // tm,),
        in_specs=[pl.BlockSpec((tm, d), lambda j: (j, 0)),
                  pl.BlockSpec((tm, NA_WIDTH), lambda j: (j, 0)),
                  pl.BlockSpec((tm, RET_WIDTH), lambda j: (j, 0)),
                  pl.BlockSpec((NA_WIDTH + RET_WIDTH, d), lambda j: (0, 0)),
                  pl.BlockSpec((1, 1, d), lambda j: (mrow(j), 0, 2)),
                  pl.BlockSpec((1, 1, d), lambda j: (mrow(j), 0, 3)),
                  pl.BlockSpec((1, 1, d), lambda j: (mrow(j), 0, 4)),
                  pl.BlockSpec((1, d), lambda j: (0, 0)),
                  pl.BlockSpec((N_EXPERTS, d), lambda j: (0, 0)),
                  pl.BlockSpec((N_EXPERTS, 1), lambda j: (0, 0)),
                  pl.BlockSpec((tm, tm), lambda j: (0, 0)),
                  pl.BlockSpec((N_EXPERTS, 1), lambda j: (0, 0))],
        out_specs=[pl.BlockSpec((tm, d), lambda j: (j, 0)),
                   pl.BlockSpec((tm, half), lambda j: (j, 0)),
                   pl.BlockSpec((2, tm), lambda j: (0, j)),
                   pl.BlockSpec((2, tm), lambda j: (0, j)),
                   pl.BlockSpec((N_EXPERTS, 1), lambda j: (0, 0))],
        out_shape=[jax.ShapeDtypeStruct((t, d), F32),
                   jax.ShapeDtypeStruct((t, half), U32),
                   jax.ShapeDtypeStruct((2, t), I32),
                   jax.ShapeDtypeStruct((2, t), F32),
                   jax.ShapeDtypeStruct((N_EXPERTS, 1), F32)],
        scratch_shapes=[pltpu.VMEM((N_EXPERTS, 1), F32)],
        compiler_params=_params(1),
        name="out_proj_router",
    )(xf, na, rt, w_out, mod3, mod3, mod3, nw, rwt, rb, tri, cnt_in)


def _row_of(pstart_ref, code):
    return pstart_ref[code >> RANK_BITS] + (code & ((1 << RANK_BITS) - 1))


def _dispatch_kernel(pstart_ref, code_ref, tok_ref, xs_in_ref, xs_ref, sem):
    del xs_in_ref
    tm = code_ref.shape[1]
    base = pl.program_id(0) * tm

    def issue(t, carry):
        for k in range(2):
            row = _row_of(pstart_ref, code_ref[k, t])
            pltpu.make_async_copy(tok_ref.at[pl.ds(base + t, 1)], xs_ref.at[pl.ds(row, 1)], sem).start()
        return carry

    lax.fori_loop(0, tm, issue, 0)
    pltpu.make_async_copy(tok_ref.at[pl.ds(0, 2 * tm)], xs_ref.at[pl.ds(0, 2 * tm)], sem).wait()


def _dispatch(pstart, code, tok, xs):
    t = tok.shape[0]
    tm = TOKEN_TILE
    return pl.pallas_call(
        _dispatch_kernel,
        grid_spec=pltpu.PrefetchScalarGridSpec(
            num_scalar_prefetch=1,
            grid=(t // tm,),
            in_specs=[pl.BlockSpec((2, tm), lambda j, ps: (0, j), memory_space=pltpu.SMEM),
                      pl.BlockSpec(memory_space=pl.ANY),
                      pl.BlockSpec(memory_space=pl.ANY)],
            out_specs=pl.BlockSpec(memory_space=pl.ANY),
            scratch_shapes=[pltpu.SemaphoreType.DMA(())]),
        out_shape=jax.ShapeDtypeStruct(xs.shape, xs.dtype),
        input_output_aliases={3: 0},
        compiler_params=_params(1),
        name="moe_dispatch",
    )(pstart, code, tok, xs)


def _ffn_kernel(bexp_ref, nvalid_ref, xs_ref, wg_ref, wu_ref, wd_ref, ys_ref):
    i = pl.program_id(0)

    @pl.when(i < nvalid_ref[0])
    def _():
        lo, hi = _unpack_bf16_pair(xs_ref[...])
        lo = lo.astype(BF16)
        hi = hi.astype(BF16)
        half = lo.shape[1]
        g = (jnp.dot(lo, wg_ref[0, 0:half, :], preferred_element_type=F32)
             + jnp.dot(hi, wg_ref[0, half:, :], preferred_element_type=F32))
        u = (jnp.dot(lo, wu_ref[0, 0:half, :], preferred_element_type=F32)
             + jnp.dot(hi, wu_ref[0, half:, :], preferred_element_type=F32))
        act = (_silu(g) * u).astype(BF16)
        ys_ref[...] = _pack_bf16_pair(jnp.dot(act, wd_ref[0], preferred_element_type=F32))

    @pl.when(i >= nvalid_ref[0])
    def _():
        ys_ref[...] = jnp.zeros_like(ys_ref)


def _ffn(bexp, nvalid, xs, wg, wu, wd):
    p, half = xs.shape
    bm = MOE_BLOCK_ROWS
    d, ff = wg.shape[1], wg.shape[2]
    live = lambda i, nv: jnp.minimum(i, nv[0] - 1)
    return pl.pallas_call(
        _ffn_kernel,
        grid_spec=pltpu.PrefetchScalarGridSpec(
            num_scalar_prefetch=2,
            grid=(p // bm,),
            in_specs=[pl.BlockSpec((bm, half), lambda i, be, nv: (live(i, nv), 0)),
                      pl.BlockSpec((1, d, ff), lambda i, be, nv: (be[i], 0, 0)),
                      pl.BlockSpec((1, d, ff), lambda i, be, nv: (be[i], 0, 0)),
                      pl.BlockSpec((1, ff, d), lambda i, be, nv: (be[i], 0, 0))],
            out_specs=pl.BlockSpec((bm, half), lambda i, be, nv: (i, 0))),
        out_shape=jax.ShapeDtypeStruct((p, half), U32),
        compiler_params=_params(1),
        name="moe_ffn",
    )(bexp, nvalid, xs, wg, wu, wd)


def _combine_kernel(pstart_ref, code_ref, x_ref, gate_ref, gf_ref, ys_ref, o_ref, ybuf, sem):
    tm = x_ref.shape[0]

    def issue(t, carry):
        for k in range(2):
            row = _row_of(pstart_ref, code_ref[k, t])
            pltpu.make_async_copy(ys_ref.at[pl.ds(row, 1)], ybuf.at[k, pl.ds(t, 1)], sem).start()
        return carry

    lax.fori_loop(0, tm, issue, 0)
    gcol = [jnp.transpose(jnp.broadcast_to(gate_ref[k:k + 1, :], (128, tm))) for k in range(2)]
    for k in range(2):
        pltpu.make_async_copy(ys_ref.at[pl.ds(0, tm)], ybuf.at[k], sem).wait()
    half = ybuf.shape[2]
    lo0, hi0 = _unpack_bf16_pair(ybuf[0])
    lo1, hi1 = _unpack_bf16_pair(ybuf[1])
    gf = gf_ref[0]
    for cidx in range(x_ref.shape[1] // 128):
        cs = slice(cidx * 128, (cidx + 1) * 128)
        if cidx * 128 < half:
            y0, y1 = lo0[:, cs], lo1[:, cs]
        else:
            hs = slice(cidx * 128 - half, (cidx + 1) * 128 - half)
            y0, y1 = hi0[:, hs], hi1[:, hs]
        o_ref[:, cs] = x_ref[:, cs] + gf[:, cs] * (gcol[0] * y0 + gcol[1] * y1)


def _combine(pstart, code, xf, gate, mod3, ys, *, tiles_per_row, ctx_row):
    t, d = xf.shape
    tm = COMBINE_TILE
    half = ys.shape[1]
    if ctx_row is None:
        mrow = lambda j, ps: j // tiles_per_row
    else:
        mrow = lambda j, ps: ctx_row
    return pl.pallas_call(
        _combine_kernel,
        grid_spec=pltpu.PrefetchScalarGridSpec(
            num_scalar_prefetch=1,
            grid=(t // tm,),
            in_specs=[pl.BlockSpec((2, tm), lambda j, ps: (0, j), memory_space=pltpu.SMEM),
                      pl.BlockSpec((tm, d), lambda j, ps: (j, 0)),
                      pl.BlockSpec((2, tm), lambda j, ps: (0, j)),
                      pl.BlockSpec((1, 1, d), lambda j, ps: (mrow(j, ps), 0, 5)),
                      pl.BlockSpec(memory_space=pl.ANY)],
            out_specs=pl.BlockSpec((tm, d), lambda j, ps: (j, 0)),
            scratch_shapes=[pltpu.VMEM((2, tm, half), U32), pltpu.SemaphoreType.DMA(())]),
        out_shape=jax.ShapeDtypeStruct((t, d), F32),
        compiler_params=_params(1),
        name="moe_combine",
    )(pstart, code, xf, gate, mod3, ys)


def _rope_tables(l):
    half = RET_HEAD_DIM // 4
    t = jnp.arange(l)
    rows = (t // GRID_W).astype(F32)
    cols = (t % GRID_W).astype(F32)
    freqs = ROPE_BASE ** (-jnp.arange(half, dtype=F32) / half)
    ang_r = rows[:, None] * freqs
    ang_c = cols[:, None] * freqs
    ang = jnp.concatenate([ang_r, ang_r, ang_c, ang_c], axis=1)
    sign = np.tile(np.concatenate([-np.ones(half), np.ones(half)]), 2).astype(np.float32)
    return jnp.cos(ang), jnp.sin(ang) * sign


def _slot_to_expert():
    r = np.arange(N_EXPERTS)
    return (r % N_GROUPS) * EXPERTS_PER_GROUP + r // N_GROUPS


def kernel(x, c, ctx, c_ctx, ada_w, ada_b, norm_mix_w, norm_ffn_w, w_in, na_q_norm, na_k_norm, na_rpb,
           ret_decay, w_out, router_w, router_bias, exp_w_gate, exp_w_up, exp_w_down):
    b, l, d = x.shape
    lc = ctx.shape[1]
    depth = ada_w.shape[0]
    assert b + 1 <= MOD_ROWS and l % (NA_Q_ROWS * GRID_W) == 0 and l // GRID_W >= NA_K_ROWS
    assert (b * l) % TOKEN_TILE == 0 and (b * lc) % TOKEN_TILE == 0 and l % TOKEN_TILE == 0

    cc = jnp.concatenate([c, c_ctx[None], jnp.zeros((MOD_ROWS - b - 1, d), F32)], axis=0)
    mods = _ada(cc, ada_w, ada_b)

    grp = jnp.asarray(np.kron(np.eye(NA_HEADS), np.full((NA_HEAD_DIM, NA_HEAD_DIM), 1.0 / NA_HEAD_DIM)), BF16)
    rope_tabs = _rope_tables(l)
    slot_exp = _slot_to_expert()
    rwt = router_w.T[slot_exp].astype(BF16)
    rb = router_bias[slot_exp].reshape(N_EXPERTS, 1).astype(F32)
    tri = jnp.asarray(np.triu(np.ones((TOKEN_TILE, TOKEN_TILE), np.float32), 1), BF16)
    bm = MOE_BLOCK_ROWS

    for layer in range(depth):
        last = layer == depth - 1
        mod3 = mods[layer].reshape(MOD_ROWS, 1, 6 * d)
        w_in_l = w_in[layer].astype(BF16)
        w_out_l = w_out[layer].astype(BF16)
        qn = jnp.tile(na_q_norm[layer], NA_HEADS)[None].astype(F32)
        kn = jnp.tile(na_k_norm[layer], NA_HEADS)[None].astype(F32)
        nw_mix = norm_mix_w[layer][None]
        nw_ffn = norm_ffn_w[layer][None]

        pc = _proj(ctx, mod3, nw_mix, w_in_l, grp, qn, kn, None, ctx_row=b)
        pq = _proj(x, mod3, nw_mix, w_in_l, grp, qn, kn, rope_tabs, ctx_row=None)
        qac, kac, vac, qbc, kbc, vbc, gfc, gbc = pc
        qa, ka, va, qb, kb, vb, gfw, gbw = pq

        zero_state = jnp.zeros((b, RET_HEADS, 2, RET_HEAD_DIM, RET_HEAD_DIM), F32)
        ret_c, ctx_state = _retention(ret_decay[layer], qbc, kbc, vbc, gfc, gbc, zero_state)
        ret_l, _ = _retention(ret_decay[layer], qb, kb, vb, gfw, gbw, ctx_state)
        na_l = _na_attention(qa, ka, va, kac, vac, _na_bias_table(na_rpb[layer]))

        cnt0 = jnp.zeros((N_EXPERTS, 1), F32)
        xf, tok_l, code_l, gate_l, cnt = _mix(
            x.reshape(b * l, d), na_l.reshape(b * l, NA_WIDTH), ret_l.reshape(b * l, RET_WIDTH), w_out_l, mod3,
            nw_ffn, rwt, rb, tri, cnt0, tiles_per_row=l // TOKEN_TILE, ctx_row=None)
        n_assign = 2 * b * l
        if not last:
            na_c = _ctx_attention(qac, kac, vac)
            cf, tok_c, code_c, gate_c, cnt = _mix(
                ctx.reshape(b * lc, d), na_c.reshape(b * lc, NA_WIDTH), ret_c.reshape(b * lc, RET_WIDTH), w_out_l,
                mod3, nw_ffn, rwt, rb, tri, cnt, tiles_per_row=1, ctx_row=b)
            n_assign += 2 * b * lc

        counts = cnt[:, 0].astype(I32)
        pcounts = (counts + bm - 1) // bm * bm
        pend = jnp.cumsum(pcounts)
        pstart = (pend - pcounts).astype(I32)
        n_rows = n_assign + N_EXPERTS * bm
        n_blocks = n_rows // bm
        nvalid = (pend[-1] // bm).astype(I32)
        blk = jnp.minimum(jnp.arange(n_blocks, dtype=I32), nvalid - 1) * bm
        blk_slot = jnp.minimum(jnp.sum(blk[:, None] >= pend[None, :], axis=1), N_EXPERTS - 1)
        blk_exp = jnp.asarray(slot_exp, I32)[blk_slot]

        xs = jnp.zeros((n_rows, d // 2), U32)
        xs = _dispatch(pstart, code_l, tok_l, xs)
        if not last:
            xs = _dispatch(pstart, code_c, tok_c, xs)
        ys = _ffn(blk_exp, nvalid.reshape(1), xs, exp_w_gate[layer].astype(BF16), exp_w_up[layer].astype(BF16),
                  exp_w_down[layer].astype(BF16))
        x = _combine(pstart, code_l, xf, gate_l, mod3, ys,
                     tiles_per_row=l // COMBINE_TILE, ctx_row=None).reshape(b, l, d)
        if not last:
            ctx = _combine(pstart, code_c, cf, gate_c, mod3, ys, tiles_per_row=1, ctx_row=b).reshape(b, lc, d)
    return x
```

```python
import functools

import numpy as np
import jax
import jax.numpy as jnp
from jax import lax
from jax.experimental import pallas as pl
from jax.experimental.pallas import tpu as pltpu

F32 = jnp.float32
BF16 = jnp.bfloat16
I32 = jnp.int32

GRID_W = 64
NA_HEAD_DIM = 64
NA_HEADS = 8
NA_WIDTH = NA_HEADS * NA_HEAD_DIM
NA_WIN_H = 8
NA_WIN_W = 16
NA_Q_ROWS = 4
NA_K_ROWS = NA_Q_ROWS + NA_WIN_H
RET_HEAD_DIM = 128
RET_HEADS = 4
RET_WIDTH = RET_HEADS * RET_HEAD_DIM
RET_CHUNK = 512
ROPE_BASE = 10000.0
N_EXPERTS = 32
N_GROUPS = 8
EXPERTS_PER_GROUP = N_EXPERTS // N_GROUPS
NORM_EPS = 1e-6
HEAD_NORM_EPS = 1e-5
NEG_INF = -1e30
MOE_BLOCK_ROWS = 256
RANK_BITS = 20
TOKEN_TILE = 512
COMBINE_TILE = 256
MOD_ROWS = 24
V7X_VMEM_LIMIT_BYTES = 56 * 1024 * 1024

_NT = (((1,), (1,)), ((), ()))
_TN = (((0,), (0,)), ((), ()))


def _params(n_axes):
    return pltpu.CompilerParams(dimension_semantics=("arbitrary",) * n_axes,
                                vmem_limit_bytes=V7X_VMEM_LIMIT_BYTES)


def _silu(v):
    return v * jax.nn.sigmoid(v)


def _ada_kernel(cc_ref, w_ref, b_ref, o_ref):
    a = _silu(cc_ref[...])
    o_ref[0] = jnp.dot(a, w_ref[0], preferred_element_type=F32, precision=lax.Precision.HIGHEST) + b_ref[0]


def _ada(cc, ada_w, ada_b):
    depth, d, n = ada_w.shape
    tn = 1536
    return pl.pallas_call(
        _ada_kernel,
        grid=(depth, n // tn),
        in_specs=[pl.BlockSpec((MOD_ROWS, d), lambda l, j: (0, 0)),
                  pl.BlockSpec((1, d, tn), lambda l, j: (l, 0, j)),
                  pl.BlockSpec((1, 1, tn), lambda l, j: (l, 0, j))],
        out_specs=pl.BlockSpec((1, MOD_ROWS, tn), lambda l, j: (l, 0, j)),
        out_shape=jax.ShapeDtypeStruct((depth, MOD_ROWS, n), F32),
        compiler_params=_params(2),
        name="ada_mod",
    )(cc, ada_w, ada_b.reshape(depth, 1, n))


def _proj_kernel(*refs, rope):
    if rope:
        (x_ref, sh_ref, sc_ref, nw_ref, w_ref, grp_ref, qn_ref, kn_ref, cos_ref, sin_ref,
         qa_ref, ka_ref, va_ref, qb_ref, kb_ref, vb_ref, gf_ref, gb_ref) = refs
    else:
        (x_ref, sh_ref, sc_ref, nw_ref, w_ref, grp_ref, qn_ref, kn_ref,
         qa_ref, ka_ref, va_ref, qb_ref, kb_ref, vb_ref, gf_ref, gb_ref) = refs
    x = x_ref[0]
    ms = jnp.mean(x * x, axis=-1, keepdims=True)
    h = x * lax.rsqrt(ms + NORM_EPS) * nw_ref[...]
    h = (h * (1.0 + sc_ref[0]) + sh_ref[0]).astype(BF16)

    def mm(g):
        return jnp.dot(h, w_ref[:, g * NA_WIDTH:(g + 1) * NA_WIDTH], preferred_element_type=F32)

    def head_rms(a, wn):
        ss = jnp.dot((a * a).astype(BF16), grp_ref[...], preferred_element_type=F32)
        return a * lax.rsqrt(ss + NORM_EPS) * wn

    def rope_fn(a):
        if not rope:
            return a
        lane = lax.broadcasted_iota(I32, (1, RET_HEAD_DIM), 1)
        first = (lane % 64) < 32
        cos = cos_ref[...]
        sin = sin_ref[...]
        parts = []
        for hh in range(RET_HEADS):
            ah = a[:, hh * RET_HEAD_DIM:(hh + 1) * RET_HEAD_DIM]
            rot = jnp.where(first, pltpu.roll(ah, RET_HEAD_DIM - 32, 1), pltpu.roll(ah, 32, 1))
            parts.append(ah * cos + rot * sin)
        return jnp.concatenate(parts, axis=1)

    qa_ref[0] = (head_rms(mm(0), qn_ref[...]) * (NA_HEAD_DIM ** -0.5)).astype(BF16)
    ka_ref[0] = head_rms(mm(1), kn_ref[...]).astype(BF16)
    va_ref[0] = mm(2).astype(BF16)
    qb_ref[0] = rope_fn(mm(3)).astype(BF16)
    kb_ref[0] = (rope_fn(mm(4)) * (RET_HEAD_DIM ** -0.5)).astype(BF16)
    vb_ref[0] = mm(5).astype(BF16)
    gf_ref[0] = _silu(mm(6)).astype(BF16)
    gb_ref[0] = _silu(mm(7)).astype(BF16)


def _proj(x, mod3, nw, w_in, grp, qn, kn, rope_tabs, *, ctx_row):
    b, lx, d = x.shape
    rope = ctx_row is None
    tm = min(TOKEN_TILE, lx)
    n_cols = w_in.shape[1]
    if rope:
        mrow = lambda i, bb: bb
    else:
        mrow = lambda i, bb: ctx_row
    in_specs = [
        pl.BlockSpec((1, tm, d), lambda i, bb: (bb, i, 0)),
        pl.BlockSpec((1, 1, d), lambda i, bb: (mrow(i, bb), 0, 0)),
        pl.BlockSpec((1, 1, d), lambda i, bb: (mrow(i, bb), 0, 1)),
        pl.BlockSpec((1, d), lambda i, bb: (0, 0)),
        pl.BlockSpec((d, n_cols), lambda i, bb: (0, 0)),
        pl.BlockSpec((NA_WIDTH, NA_WIDTH), lambda i, bb: (0, 0)),
        pl.BlockSpec((1, NA_WIDTH), lambda i, bb: (0, 0)),
        pl.BlockSpec((1, NA_WIDTH), lambda i, bb: (0, 0)),
    ]
    args = [x, mod3, mod3, nw, w_in, grp, qn, kn]
    if rope:
        in_specs += [pl.BlockSpec((tm, RET_HEAD_DIM), lambda i, bb: (i, 0))] * 2
        args += list(rope_tabs)
    out_spec = pl.BlockSpec((1, tm, NA_WIDTH), lambda i, bb: (bb, i, 0))
    out_shape = jax.ShapeDtypeStruct((b, lx, NA_WIDTH), BF16)
    return pl.pallas_call(
        functools.partial(_proj_kernel, rope=rope),
        grid=(lx // tm, b),
        in_specs=in_specs,
        out_specs=[out_spec] * 8,
        out_shape=[out_shape] * 8,
        compiler_params=_params(2),
        name="in_proj_lat" if rope else "in_proj_ctx",
    )(*args)


def _attn_kernel(*refs, latent, n_steps):
    if latent:
        q_ref, k_ref, v_ref, kc_ref, vc_ref, bias_ref, o_ref = refs
        i = pl.program_id(1)
        rows = n_steps * NA_Q_ROWS
        ws = jnp.clip(i * NA_Q_ROWS - NA_WIN_H // 2, 0, rows - NA_K_ROWS)
        kstart = pl.multiple_of(ws * GRID_W, GRID_W)
    else:
        q_ref, kc_ref, vc_ref, o_ref = refs
    lane = lax.broadcasted_iota(I32, (1, 2 * NA_HEAD_DIM), 1)
    lo = lane < NA_HEAD_DIM
    for j in range(NA_HEADS // 2):
        cs = slice(j * 2 * NA_HEAD_DIM, (j + 1) * 2 * NA_HEAD_DIM)
        qp = q_ref[0, :, cs]
        kcp = kc_ref[0, :, cs]
        vcp = vc_ref[0, :, cs]
        if latent:
            kp = k_ref[0, pl.ds(kstart, NA_K_ROWS * GRID_W), cs]
            vp = v_ref[0, pl.ds(kstart, NA_K_ROWS * GRID_W), cs]
        halves = []
        for half in range(2):
            qm = jnp.where(lo if half == 0 else jnp.logical_not(lo), qp, jnp.zeros_like(qp))
            s_ctx = lax.dot_general(qm, kcp, _NT, preferred_element_type=F32)
            m = jnp.max(s_ctx, axis=-1, keepdims=True)
            if latent:
                s_lat = lax.dot_general(qm, kp, _NT, preferred_element_type=F32) + bias_ref[0, 2 * j + half]
                m = jnp.maximum(m, jnp.max(s_lat, axis=-1, keepdims=True))
            p_ctx = jnp.exp(s_ctx - m)
            den = jnp.sum(p_ctx, axis=-1, keepdims=True)
            o = jnp.dot(p_ctx.astype(BF16), vcp, preferred_element_type=F32)
            if latent:
                p_lat = jnp.exp(s_lat - m)
                den = den + jnp.sum(p_lat, axis=-1, keepdims=True)
                o = o + jnp.dot(p_lat.astype(BF16), vp, preferred_element_type=F32)
            halves.append(o / den)
        o_ref[0, :, cs] = jnp.where(lo, halves[0], halves[1]).astype(BF16)


def _na_attention(qa, ka, va, kc, vc, bias):
    b, l, w = qa.shape
    lc = kc.shape[1]
    tq = NA_Q_ROWS * GRID_W
    tk = NA_K_ROWS * GRID_W
    n_steps = l // tq
    case = lambda i: jnp.where(i == 0, 0, jnp.where(i == n_steps - 1, 2, 1))
    return pl.pallas_call(
        functools.partial(_attn_kernel, latent=True, n_steps=n_steps),
        grid=(b, n_steps),
        in_specs=[pl.BlockSpec((1, tq, w), lambda bb, i: (bb, i, 0)),
                  pl.BlockSpec((1, l, w), lambda bb, i: (bb, 0, 0)),
                  pl.BlockSpec((1, l, w), lambda bb, i: (bb, 0, 0)),
                  pl.BlockSpec((1, lc, w), lambda bb, i: (bb, 0, 0)),
                  pl.BlockSpec((1, lc, w), lambda bb, i: (bb, 0, 0)),
                  pl.BlockSpec((1, NA_HEADS, tq, tk), lambda bb, i: (case(i), 0, 0, 0))],
        out_specs=pl.BlockSpec((1, tq, w), lambda bb, i: (bb, i, 0)),
        out_shape=jax.ShapeDtypeStruct((b, l, w), BF16),
        compiler_params=_params(2),
        name="na_attention",
    )(qa, ka, va, kc, vc, bias)


def _ctx_attention(qc, kc, vc):
    b, lc, w = qc.shape
    spec = pl.BlockSpec((1, lc, w), lambda bb: (bb, 0, 0))
    return pl.pallas_call(
        functools.partial(_attn_kernel, latent=False, n_steps=1),
        grid=(b,),
        in_specs=[spec, spec, spec],
        out_specs=spec,
        out_shape=jax.ShapeDtypeStruct((b, lc, w), BF16),
        compiler_params=_params(1),
        name="ctx_attention",
    )(qc, kc, vc)


def _na_bias_table(rpb):
    h = rpb.shape[0]
    qcol = np.arange(GRID_W)
    kcol = np.arange(GRID_W)
    c0 = np.clip(qcol - NA_WIN_W // 2, 0, GRID_W - NA_WIN_W)
    col_ok = (kcol[None, :] >= c0[:, None]) & (kcol[None, :] < c0[:, None] + NA_WIN_W)
    dc = np.clip(kcol[None, :] - qcol[:, None] + NA_WIN_W - 1, 0, 2 * NA_WIN_W - 2)
    by_dr = jnp.where(jnp.asarray(col_ok)[None, None], rpb[:, :, jnp.asarray(dc)], NEG_INF)
    masked = jnp.full((h, GRID_W, GRID_W), NEG_INF, F32)
    cases = []
    for lo_fn, dr_off in ((lambda qr: 0, NA_WIN_H - 1),
                          (lambda qr: qr, NA_WIN_H // 2 - 1),
                          (lambda qr: NA_K_ROWS - NA_WIN_H, -1)):
        rows = []
        for qr in range(NA_Q_ROWS):
            blocks = []
            for kr in range(NA_K_ROWS):
                ok = lo_fn(qr) <= kr < lo_fn(qr) + NA_WIN_H
                blocks.append(by_dr[:, kr - qr + dr_off] if ok else masked)
            rows.append(jnp.concatenate(blocks, axis=2))
        cases.append(jnp.concatenate(rows, axis=1))
    return jnp.stack(cases, axis=0)


def _ret_kernel(rd_ref, q_ref, k_ref, v_ref, gf_ref, gb_ref, s0_ref, o_ref, sfin_ref, accf_ref, accb_ref,
                intra_ref, dec_ref, *, c, n_chunks):
    hd = pl.program_id(0)
    log_g_all = -jnp.exp(rd_ref[...])
    head = lax.broadcasted_iota(I32, log_g_all.shape, 1)
    log_g = jnp.sum(jnp.where(head == hd, log_g_all, 0.0), axis=1, keepdims=True)
    lg = (log_g[0:1, :], log_g[1:2, :])

    @pl.when(pl.program_id(1) == 0)
    def _():
        diff = (lax.broadcasted_iota(I32, (c, c), 0) - lax.broadcasted_iota(I32, (c, c), 1)).astype(F32)
        pos = lax.broadcasted_iota(I32, (c, RET_HEAD_DIM), 0).astype(F32)
        intra_ref[0] = jnp.where(diff >= 0, jnp.exp(jnp.maximum(diff, 0.0) * lg[0]), 0.0)
        dec_ref[0] = jnp.exp((pos + 1.0) * lg[0])
        dec_ref[1] = jnp.exp((c - 1.0 - pos) * lg[0])
        intra_ref[1] = jnp.where(diff <= 0, jnp.exp(jnp.maximum(-diff, 0.0) * lg[1]), 0.0)
        dec_ref[2] = jnp.exp((c - pos) * lg[1])
        dec_ref[3] = jnp.exp(pos * lg[1])

    def direction(d, gate_ref, part_ref):
        c_dec = jnp.exp(c * lg[d])

        def step(t, s):
            n = t if d == 0 else n_chunks - 1 - t
            off = pl.multiple_of(n * c, c)
            qb = q_ref[0, pl.ds(off, c), :]
            kb = k_ref[0, pl.ds(off, c), :]
            vb = v_ref[0, pl.ds(off, c), :]
            scores = lax.dot_general(qb, kb, _NT, preferred_element_type=F32) * intra_ref[d]
            o = (jnp.dot(scores.astype(BF16), vb, preferred_element_type=F32)
                 + jnp.dot(qb, s.astype(BF16), preferred_element_type=F32) * dec_ref[2 * d])
            kd = (kb.astype(F32) * dec_ref[2 * d + 1]).astype(BF16)
            s = s * c_dec + lax.dot_general(kd, vb, _TN, preferred_element_type=F32)
            mu = jnp.mean(o, axis=-1, keepdims=True)
            oc = o - mu
            var = jnp.mean(oc * oc, axis=-1, keepdims=True)
            part_ref[pl.ds(off, c), :] = (oc * lax.rsqrt(var + HEAD_NORM_EPS)
                                          * gate_ref[0, pl.ds(off, c), :].astype(F32))
            return s

        return step

    step_f = direction(0, gf_ref, accf_ref)
    step_b = direction(1, gb_ref, accb_ref)

    def both(t, carry):
        return step_f(t, carry[0]), step_b(t, carry[1])

    s_f, s_b = lax.fori_loop(0, n_chunks, both, (s0_ref[0, 0, 0], s0_ref[0, 0, 1]), unroll=min(2, n_chunks))
    sfin_ref[0, 0, 0] = s_f
    sfin_ref[0, 0, 1] = s_b

    def combine(i, carry):
        off = pl.multiple_of(i * c, c)
        o_ref[0, pl.ds(off, c), :] = (accf_ref[pl.ds(off, c), :] + accb_ref[pl.ds(off, c), :]).astype(BF16)
        return carry

    lax.fori_loop(0, n_chunks, combine, 0)


def _retention(ret_decay, q, k, v, gf, gb, s0):
    b, lx, w = q.shape
    hd = RET_HEAD_DIM
    c = min(RET_CHUNK, lx)
    seq = pl.BlockSpec((1, lx, hd), lambda h, bb: (bb, 0, h))
    st = pl.BlockSpec((1, 1, 2, hd, hd), lambda h, bb: (bb, h, 0, 0, 0))
    return pl.pallas_call(
        functools.partial(_ret_kernel, c=c, n_chunks=lx // c),
        grid=(RET_HEADS, b),
        in_specs=[pl.BlockSpec((2, RET_HEADS), lambda h, bb: (0, 0)), seq, seq, seq, seq, seq, st],
        out_specs=[seq, st],
        out_shape=[jax.ShapeDtypeStruct((b, lx, w), BF16),
                   jax.ShapeDtypeStruct((b, RET_HEADS, 2, hd, hd), F32)],
        scratch_shapes=[pltpu.VMEM((lx, hd), F32), pltpu.VMEM((lx, hd), F32), pltpu.VMEM((2, c, c), F32),
                        pltpu.VMEM((4, c, hd), F32)],
        compiler_params=_params(2),
        name="retention",
    )(ret_decay, q, k, v, gf, gb, s0)


def _mix_kernel(x_ref, na_ref, rt_ref, wo_ref, ga_ref, shf_ref, scf_ref, nw_ref, rwt_ref, rb_ref, tri_ref,
                cnt_in_ref, xo_ref, tok_ref, code_ref, gate_ref, cnt_out_ref, cnt_sc):
    @pl.when(pl.program_id(0) == 0)
    def _():
        cnt_sc[...] = cnt_in_ref[...]

    mix = (jnp.dot(na_ref[...], wo_ref[0:NA_WIDTH, :], preferred_element_type=F32)
           + jnp.dot(rt_ref[...], wo_ref[NA_WIDTH:, :], preferred_element_type=F32))
    x = x_ref[...] + ga_ref[0] * mix
    xo_ref[...] = x
    ms = jnp.mean(x * x, axis=-1, keepdims=True)
    t = x * lax.rsqrt(ms + NORM_EPS) * nw_ref[...]
    t = t * (1.0 + scf_ref[0]) + shf_ref[0]
    tok_ref[...] = t

    tm = x.shape[0]
    logits = lax.dot_general(rwt_ref[...], t.astype(BF16), _NT, preferred_element_type=F32)
    s = jax.nn.sigmoid(logits)
    s_sel = s + rb_ref[...]
    a = [s_sel[j * N_GROUPS:(j + 1) * N_GROUPS, :] for j in range(EXPERTS_PER_GROUP)]
    u = [s[j * N_GROUPS:(j + 1) * N_GROUPS, :] for j in range(EXPERTS_PER_GROUP)]
    hi01, lo01 = jnp.maximum(a[0], a[1]), jnp.minimum(a[0], a[1])
    hi23, lo23 = jnp.maximum(a[2], a[3]), jnp.minimum(a[2], a[3])
    gscore = jnp.maximum(hi01, hi23) + jnp.maximum(jnp.minimum(hi01, hi23), jnp.maximum(lo01, lo23))
    gidx = lax.broadcasted_iota(I32, (N_GROUPS, tm), 0)
    gmax = jnp.max(gscore, axis=0, keepdims=True)
    grp = jnp.min(jnp.where(gscore == gmax, gidx.astype(F32), float(N_GROUPS)), axis=0,
                  keepdims=True).astype(I32)
    sel = gidx == grp
    av = [jnp.sum(jnp.where(sel, a[j], 0.0), axis=0, keepdims=True) for j in range(EXPERTS_PER_GROUP)]
    uv = [jnp.sum(jnp.where(sel, u[j], 0.0), axis=0, keepdims=True) for j in range(EXPERTS_PER_GROUP)]
    best1, loc1 = av[0], jnp.zeros((1, tm), I32)
    for j in range(1, EXPERTS_PER_GROUP):
        take = av[j] > best1
        best1 = jnp.where(take, av[j], best1)
        loc1 = jnp.where(take, j, loc1)
    best2, loc2 = jnp.full((1, tm), -jnp.inf, F32), jnp.full((1, tm), -1, I32)
    for j in range(EXPERTS_PER_GROUP):
        take = jnp.logical_and(loc1 != j, jnp.logical_or(loc2 < 0, av[j] > best2))
        best2 = jnp.where(take, av[j], best2)
        loc2 = jnp.where(take, j, loc2)
    g1 = sum(jnp.where(loc1 == j, uv[j], 0.0) for j in range(EXPERTS_PER_GROUP))
    g2 = sum(jnp.where(loc2 == j, uv[j], 0.0) for j in range(EXPERTS_PER_GROUP))
    gate_ref[0:1, :] = g1 / (g1 + g2)
    gate_ref[1:2, :] = g2 / (g1 + g2)

    r1 = loc1 * N_GROUPS + grp
    r2 = loc2 * N_GROUPS + grp
    ridx = lax.broadcasted_iota(I32, (N_EXPERTS, tm), 0)
    hit1 = ridx == r1
    hit2 = ridx == r2
    onehot = jnp.logical_or(hit1, hit2).astype(F32)
    before = jnp.dot(onehot.astype(BF16), tri_ref[...], preferred_element_type=F32) + cnt_sc[...]
    rank1 = jnp.sum(jnp.where(hit1, before, 0.0), axis=0, keepdims=True).astype(I32)
    rank2 = jnp.sum(jnp.where(hit2, before, 0.0), axis=0, keepdims=True).astype(I32)
    code_ref[0:1, :] = (r1 << RANK_BITS) | rank1
    code_ref[1:2, :] = (r2 << RANK_BITS) | rank2
    cnt_sc[...] = cnt_sc[...] + jnp.sum(onehot, axis=1, keepdims=True)
    cnt_out_ref[...] = cnt_sc[...]


def _mix(xf, na, rt, w_out, mod3, nw, rwt, rb, tri, cnt_in, *, tiles_per_row, ctx_row):
    t, d = xf.shape
    tm = TOKEN_TILE
    if ctx_row is None:
        mrow = lambda j: j // tiles_per_row
    else:
        mrow = lambda j: ctx_row
    return pl.pallas_call(
        _mix_kernel,
        grid=(t // tm,),
        in_specs=[pl.BlockSpec((tm, d), lambda j: (j, 0)),
                  pl.BlockSpec((tm, NA_WIDTH), lambda j: (j, 0)),
                  pl.BlockSpec((tm, RET_WIDTH), lambda j: (j, 0)),
                  pl.BlockSpec((NA_WIDTH + RET_WIDTH, d), lambda j: (0, 0)),
                  pl.BlockSpec((1, 1, d), lambda j: (mrow(j), 0, 2)),
                  pl.BlockSpec((1, 1, d), lambda j: (mrow(j), 0, 3)),
                  pl.BlockSpec((1, 1, d), lambda j: (mrow(j), 0, 4)),
                  pl.BlockSpec((1, d), lambda j: (0, 0)),
                  pl.BlockSpec((N_EXPERTS, d), lambda j: (0, 0)),
                  pl.BlockSpec((N_EXPERTS, 1), lambda j: (0, 0)),
                  pl.BlockSpec((tm, tm), lambda j: (0, 0)),
                  pl.BlockSpec((N_EXPERTS, 1), lambda j: (0, 0))],
        out_specs=[pl.BlockSpec((tm, d), lambda j: (j, 0)),
                   pl.BlockSpec((tm, d), lambda j: (j, 0)),
                   pl.BlockSpec((2, tm), lambda j: (0, j)),
                   pl.BlockSpec((2, tm), lambda j: (0, j)),
                   pl.BlockSpec((N_EXPERTS, 1), lambda j: (0, 0))],
        out_shape=[jax.ShapeDtypeStruct((t, d), F32),
                   jax.ShapeDtypeStruct((t, d), F32),
                   jax.ShapeDtypeStruct((2, t), I32),
                   jax.ShapeDtypeStruct((2, t), F32),
                   jax.ShapeDtypeStruct((N_EXPERTS, 1), F32)],
        scratch_shapes=[pltpu.VMEM((N_EXPERTS, 1), F32)],
        compiler_params=_params(1),
        name="out_proj_router",
    )(xf, na, rt, w_out, mod3, mod3, mod3, nw, rwt, rb, tri, cnt_in)


def _rows_kernel(pstart_ref, code_ref, rows_ref):
    code = code_ref[...]
    slot = code >> RANK_BITS
    rows = code & ((1 << RANK_BITS) - 1)
    for e in range(N_EXPERTS):
        rows = rows + jnp.where(slot == e, pstart_ref[e], 0)
    rows_ref[...] = rows


def _sorted_rows(pstart, code):
    t = code.shape[1]
    tp = min(t, 4096)
    return pl.pallas_call(
        _rows_kernel,
        grid_spec=pltpu.PrefetchScalarGridSpec(
            num_scalar_prefetch=1,
            grid=(t // tp,),
            in_specs=[pl.BlockSpec((2, tp), lambda j, ps: (0, j))],
            out_specs=pl.BlockSpec((2, tp), lambda j, ps: (0, j))),
        out_shape=jax.ShapeDtypeStruct((2, t), I32),
        compiler_params=_params(1),
        name="moe_rows",
    )(pstart, code)


SMEM_LANES = 128
SUBLANES = 8


def _row_copies(rows_ref, n_tokens, copy):
    groups = SMEM_LANES // SUBLANES
    for blk in range(n_tokens // SMEM_LANES):
        def issue(i, carry, blk=blk):
            base = pl.multiple_of(i * SUBLANES, SUBLANES)
            for u in range(SUBLANES):
                for k in range(2):
                    copy(k, blk * groups + i, u, rows_ref[k, blk, 0, base + u]).start(priority=k)
            return carry

        lax.fori_loop(0, groups, issue, 0)


def _rows_spec(tm):
    return pl.BlockSpec((2, tm // SMEM_LANES, 1, SMEM_LANES), lambda j: (0, j, 0, 0), memory_space=pltpu.SMEM)


def _dispatch_kernel(rows_ref, tok_ref, xs_in_ref, xs_ref, sem):
    del xs_in_ref
    tm = tok_ref.shape[0] * SUBLANES
    _row_copies(rows_ref, tm, lambda k, g, u, row: pltpu.make_async_copy(
        tok_ref.at[g, pl.ds(u, 1)], xs_ref.at[pl.ds(row, 1)], sem))
    for _ in range(2):
        pltpu.make_async_copy(xs_ref.at[pl.ds(0, tm)], xs_ref.at[pl.ds(0, tm)], sem).wait()


def _dispatch(rows, tok, xs):
    t, d = tok.shape
    tm = TOKEN_TILE
    return pl.pallas_call(
        _dispatch_kernel,
        grid=(t // tm,),
        in_specs=[_rows_spec(tm),
                  pl.BlockSpec((tm // SUBLANES, SUBLANES, d), lambda j: (j, 0, 0)),
                  pl.BlockSpec(memory_space=pl.ANY)],
        out_specs=pl.BlockSpec(memory_space=pl.ANY),
        scratch_shapes=[pltpu.SemaphoreType.DMA(())],
        out_shape=jax.ShapeDtypeStruct(xs.shape, xs.dtype),
        input_output_aliases={2: 0},
        compiler_params=_params(1),
        name="moe_dispatch",
    )(rows.reshape(2, t // SMEM_LANES, 1, SMEM_LANES), tok.reshape(t // SUBLANES, SUBLANES, d), xs)


def _ffn_kernel(bexp_ref, nvalid_ref, xs_ref, wg_ref, wu_ref, wd_ref, ys_ref):
    i = pl.program_id(0)

    @pl.when(i < nvalid_ref[0])
    def _():
        xb = xs_ref[...].astype(BF16)
        g = jnp.dot(xb, wg_ref[0], preferred_element_type=F32)
        u = jnp.dot(xb, wu_ref[0], preferred_element_type=F32)
        act = (_silu(g) * u).astype(BF16)
        ys_ref[...] = jnp.dot(act, wd_ref[0], preferred_element_type=F32)

    @pl.when(i >= nvalid_ref[0])
    def _():
        ys_ref[...] = jnp.zeros_like(ys_ref)


def _ffn(bexp, nvalid, xs, wg, wu, wd):
    p, d = xs.shape
    bm = MOE_BLOCK_ROWS
    ff = wg.shape[2]
    live = lambda i, nv: jnp.minimum(i, nv[0] - 1)
    return pl.pallas_call(
        _ffn_kernel,
        grid_spec=pltpu.PrefetchScalarGridSpec(
            num_scalar_prefetch=2,
            grid=(p // bm,),
            in_specs=[pl.BlockSpec((bm, d), lambda i, be, nv: (live(i, nv), 0)),
                      pl.BlockSpec((1, d, ff), lambda i, be, nv: (be[i], 0, 0)),
                      pl.BlockSpec((1, d, ff), lambda i, be, nv: (be[i], 0, 0)),
                      pl.BlockSpec((1, ff, d), lambda i, be, nv: (be[i], 0, 0))],
            out_specs=pl.BlockSpec((bm, d), lambda i, be, nv: (i, 0))),
        out_shape=jax.ShapeDtypeStruct((p, d), F32),
        compiler_params=_params(1),
        name="moe_ffn",
    )(bexp, nvalid, xs, wg, wu, wd)


def _combine_kernel(rows_ref, x_ref, gate_ref, gf_ref, ys_ref, o_ref, ybuf, sem):
    tm = x_ref.shape[0]
    _row_copies(rows_ref, tm, lambda k, g, u, row: pltpu.make_async_copy(
        ys_ref.at[pl.ds(row, 1)], ybuf.at[k, g, pl.ds(u, 1)], sem))
    gcol = [jnp.transpose(jnp.broadcast_to(gate_ref[k:k + 1, :], (128, tm))) for k in range(2)]
    for k in range(2):
        pltpu.make_async_copy(ys_ref.at[pl.ds(0, tm)], ys_ref.at[pl.ds(0, tm)], sem).wait()
    gf = gf_ref[0]
    for cidx in range(x_ref.shape[1] // 128):
        cs = slice(cidx * 128, (cidx + 1) * 128)
        y0 = ybuf[0, :, :, cs].reshape(tm, 128)
        y1 = ybuf[1, :, :, cs].reshape(tm, 128)
        o_ref[:, cs] = x_ref[:, cs] + gf[:, cs] * (gcol[0] * y0 + gcol[1] * y1)


def _combine(rows, xf, gate, mod3, ys, *, tiles_per_row, ctx_row):
    t, d = xf.shape
    tm = COMBINE_TILE
    if ctx_row is None:
        mrow = lambda j: j // tiles_per_row
    else:
        mrow = lambda j: ctx_row
    return pl.pallas_call(
        _combine_kernel,
        grid=(t // tm,),
        in_specs=[_rows_spec(tm),
                  pl.BlockSpec((tm, d), lambda j: (j, 0)),
                  pl.BlockSpec((2, tm), lambda j: (0, j)),
                  pl.BlockSpec((1, 1, d), lambda j: (mrow(j), 0, 5)),
                  pl.BlockSpec(memory_space=pl.ANY)],
        out_specs=pl.BlockSpec((tm, d), lambda j: (j, 0)),
        scratch_shapes=[pltpu.VMEM((2, tm // SUBLANES, SUBLANES, d), F32), pltpu.SemaphoreType.DMA(())],
        out_shape=jax.ShapeDtypeStruct((t, d), F32),
        compiler_params=_params(1),
        name="moe_combine",
    )(rows.reshape(2, t // SMEM_LANES, 1, SMEM_LANES), xf, gate, mod3, ys)


def _rope_tables(l):
    half = RET_HEAD_DIM // 4
    t = jnp.arange(l)
    rows = (t // GRID_W).astype(F32)
    cols = (t % GRID_W).astype(F32)
    freqs = ROPE_BASE ** (-jnp.arange(half, dtype=F32) / half)
    ang_r = rows[:, None] * freqs
    ang_c = cols[:, None] * freqs
    ang = jnp.concatenate([ang_r, ang_r, ang_c, ang_c], axis=1)
    sign = np.tile(np.concatenate([-np.ones(half), np.ones(half)]), 2).astype(np.float32)
    return jnp.cos(ang), jnp.sin(ang) * sign


def _slot_to_expert():
    r = np.arange(N_EXPERTS)
    return (r % N_GROUPS) * EXPERTS_PER_GROUP + r // N_GROUPS


def kernel(x, c, ctx, c_ctx, ada_w, ada_b, norm_mix_w, norm_ffn_w, w_in, na_q_norm, na_k_norm, na_rpb,
           ret_decay, w_out, router_w, router_bias, exp_w_gate, exp_w_up, exp_w_down):
    b, l, d = x.shape
    lc = ctx.shape[1]
    depth = ada_w.shape[0]
    assert b + 1 <= MOD_ROWS and l % (NA_Q_ROWS * GRID_W) == 0 and l // GRID_W >= NA_K_ROWS
    assert (b * l) % TOKEN_TILE == 0 and (b * lc) % TOKEN_TILE == 0 and l % TOKEN_TILE == 0

    cc = jnp.concatenate([c, c_ctx[None], jnp.zeros((MOD_ROWS - b - 1, d), F32)], axis=0)
    mods = _ada(cc, ada_w, ada_b)

    grp = jnp.asarray(np.kron(np.eye(NA_HEADS), np.full((NA_HEAD_DIM, NA_HEAD_DIM), 1.0 / NA_HEAD_DIM)), BF16)
    rope_tabs = _rope_tables(l)
    slot_exp = _slot_to_expert()
    rwt = router_w.T[slot_exp].astype(BF16)
    rb = router_bias[slot_exp].reshape(N_EXPERTS, 1).astype(F32)
    tri = jnp.asarray(np.triu(np.ones((TOKEN_TILE, TOKEN_TILE), np.float32), 1), BF16)
    bm = MOE_BLOCK_ROWS

    for layer in range(depth):
        last = layer == depth - 1
        mod3 = mods[layer].reshape(MOD_ROWS, 1, 6 * d)
        w_in_l = w_in[layer].astype(BF16)
        w_out_l = w_out[layer].astype(BF16)
        qn = jnp.tile(na_q_norm[layer], NA_HEADS)[None].astype(F32)
        kn = jnp.tile(na_k_norm[layer], NA_HEADS)[None].astype(F32)
        nw_mix = norm_mix_w[layer][None]
        nw_ffn = norm_ffn_w[layer][None]

        pc = _proj(ctx, mod3, nw_mix, w_in_l, grp, qn, kn, None, ctx_row=b)
        pq = _proj(x, mod3, nw_mix, w_in_l, grp, qn, kn, rope_tabs, ctx_row=None)
        qac, kac, vac, qbc, kbc, vbc, gfc, gbc = pc
        qa, ka, va, qb, kb, vb, gfw, gbw = pq

        zero_state = jnp.zeros((b, RET_HEADS, 2, RET_HEAD_DIM, RET_HEAD_DIM), F32)
        ret_c, ctx_state = _retention(ret_decay[layer], qbc, kbc, vbc, gfc, gbc, zero_state)
        ret_l, _ = _retention(ret_decay[layer], qb, kb, vb, gfw, gbw, ctx_state)
        na_l = _na_attention(qa, ka, va, kac, vac, _na_bias_table(na_rpb[layer]))

        cnt0 = jnp.zeros((N_EXPERTS, 1), F32)
        xf, tok_l, code_l, gate_l, cnt = _mix(
            x.reshape(b * l, d), na_l.reshape(b * l, NA_WIDTH), ret_l.reshape(b * l, RET_WIDTH), w_out_l, mod3,
            nw_ffn, rwt, rb, tri, cnt0, tiles_per_row=l // TOKEN_TILE, ctx_row=None)
        n_assign = 2 * b * l
        if not last:
            na_c = _ctx_attention(qac, kac, vac)
            cf, tok_c, code_c, gate_c, cnt = _mix(
                ctx.reshape(b * lc, d), na_c.reshape(b * lc, NA_WIDTH), ret_c.reshape(b * lc, RET_WIDTH), w_out_l,
                mod3, nw_ffn, rwt, rb, tri, cnt, tiles_per_row=1, ctx_row=b)
            n_assign += 2 * b * lc

        counts = cnt[:, 0].astype(I32)
        pcounts = (counts + bm - 1) // bm * bm
        pend = jnp.cumsum(pcounts)
        pstart = (pend - pcounts).astype(I32)
        n_rows = n_assign + N_EXPERTS * bm
        n_blocks = n_rows // bm
        nvalid = (pend[-1] // bm).astype(I32)
        blk = jnp.minimum(jnp.arange(n_blocks, dtype=I32), nvalid - 1) * bm
        blk_slot = jnp.minimum(jnp.sum(blk[:, None] >= pend[None, :], axis=1), N_EXPERTS - 1)
        blk_exp = jnp.asarray(slot_exp, I32)[blk_slot]

        rows_l = _sorted_rows(pstart, code_l)
        xs = jnp.zeros((n_rows, d), F32)
        xs = _dispatch(rows_l, tok_l, xs)
        if not last:
            rows_c = _sorted_rows(pstart, code_c)
            xs = _dispatch(rows_c, tok_c, xs)
        ys = _ffn(blk_exp, nvalid.reshape(1), xs, exp_w_gate[layer].astype(BF16), exp_w_up[layer].astype(BF16),
                  exp_w_down[layer].astype(BF16))
        x = _combine(rows_l, xf, gate_l, mod3, ys, tiles_per_row=l // COMBINE_TILE, ctx_row=None).reshape(b, l, d)
        if not last:
            ctx = _combine(rows_c, cf, gate_c, mod3, ys, tiles_per_row=1, ctx_row=b).reshape(b, lc, d)
    return x
```

```python
import functools

import numpy as np
import jax
import jax.numpy as jnp
from jax import lax
from jax.experimental import pallas as pl
from jax.experimental.pallas import tpu as pltpu

F32 = jnp.float32
BF16 = jnp.bfloat16
I32 = jnp.int32

GRID_W = 64
NA_HEAD_DIM = 64
NA_HEADS = 8
NA_WIDTH = NA_HEADS * NA_HEAD_DIM
NA_WIN_H = 8
NA_WIN_W = 16
NA_Q_ROWS = 4
NA_K_ROWS = NA_Q_ROWS + NA_WIN_H
RET_HEAD_DIM = 128
RET_HEADS = 4
RET_WIDTH = RET_HEADS * RET_HEAD_DIM
RET_CHUNK = 512
ROPE_BASE = 10000.0
N_EXPERTS = 32
N_GROUPS = 8
EXPERTS_PER_GROUP = N_EXPERTS // N_GROUPS
NORM_EPS = 1e-6
HEAD_NORM_EPS = 1e-5
NEG_INF = -1e30
MOE_BLOCK_ROWS = 256
RANK_BITS = 20
TOKEN_TILE = 512
COMBINE_TILE = 256
MOD_ROWS = 24
V7X_VMEM_LIMIT_BYTES = 56 * 1024 * 1024

_NT = (((1,), (1,)), ((), ()))
_TN = (((0,), (0,)), ((), ()))


def _params(n_axes):
    return pltpu.CompilerParams(dimension_semantics=("arbitrary",) * n_axes,
                                vmem_limit_bytes=V7X_VMEM_LIMIT_BYTES)


def _silu(v):
    return v * jax.nn.sigmoid(v)


def _ada_kernel(cc_ref, w_ref, b_ref, o_ref):
    a = _silu(cc_ref[...])
    o_ref[0] = jnp.dot(a, w_ref[0], preferred_element_type=F32, precision=lax.Precision.HIGHEST) + b_ref[0]


def _ada(cc, ada_w, ada_b):
    depth, d, n = ada_w.shape
    tn = 1536
    return pl.pallas_call(
        _ada_kernel,
        grid=(depth, n // tn),
        in_specs=[pl.BlockSpec((MOD_ROWS, d), lambda l, j: (0, 0)),
                  pl.BlockSpec((1, d, tn), lambda l, j: (l, 0, j)),
                  pl.BlockSpec((1, 1, tn), lambda l, j: (l, 0, j))],
        out_specs=pl.BlockSpec((1, MOD_ROWS, tn), lambda l, j: (l, 0, j)),
        out_shape=jax.ShapeDtypeStruct((depth, MOD_ROWS, n), F32),
        compiler_params=_params(2),
        name="ada_mod",
    )(cc, ada_w, ada_b.reshape(depth, 1, n))


def _proj_kernel(*refs, rope):
    if rope:
        (x_ref, sh_ref, sc_ref, nw_ref, w_ref, grp_ref, qn_ref, kn_ref, cos_ref, sin_ref,
         qa_ref, ka_ref, va_ref, qb_ref, kb_ref, vb_ref, gf_ref, gb_ref) = refs
    else:
        (x_ref, sh_ref, sc_ref, nw_ref, w_ref, grp_ref, qn_ref, kn_ref,
         qa_ref, ka_ref, va_ref, qb_ref, kb_ref, vb_ref, gf_ref, gb_ref) = refs
    x = x_ref[0]
    ms = jnp.mean(x * x, axis=-1, keepdims=True)
    h = x * lax.rsqrt(ms + NORM_EPS) * nw_ref[...]
    h = (h * (1.0 + sc_ref[0]) + sh_ref[0]).astype(BF16)

    def mm(g):
        return jnp.dot(h, w_ref[:, g * NA_WIDTH:(g + 1) * NA_WIDTH], preferred_element_type=F32)

    def head_rms(a, wn):
        ss = jnp.dot((a * a).astype(BF16), grp_ref[...], preferred_element_type=F32)
        return a * lax.rsqrt(ss + NORM_EPS) * wn

    def rope_fn(a):
        if not rope:
            return a
        lane = lax.broadcasted_iota(I32, (1, RET_HEAD_DIM), 1)
        first = (lane % 64) < 32
        cos = cos_ref[...]
        sin = sin_ref[...]
        parts = []
        for hh in range(RET_HEADS):
            ah = a[:, hh * RET_HEAD_DIM:(hh + 1) * RET_HEAD_DIM]
            rot = jnp.where(first, pltpu.roll(ah, RET_HEAD_DIM - 32, 1), pltpu.roll(ah, 32, 1))
            parts.append(ah * cos + rot * sin)
        return jnp.concatenate(parts, axis=1)

    qa_ref[0] = (head_rms(mm(0), qn_ref[...]) * (NA_HEAD_DIM ** -0.5)).astype(BF16)
    ka_ref[0] = head_rms(mm(1), kn_ref[...]).astype(BF16)
    va_ref[0] = mm(2).astype(BF16)
    qb_ref[0] = rope_fn(mm(3)).astype(BF16)
    kb_ref[0] = (rope_fn(mm(4)) * (RET_HEAD_DIM ** -0.5)).astype(BF16)
    vb_ref[0] = mm(5).astype(BF16)
    gf_ref[0] = _silu(mm(6)).astype(BF16)
    gb_ref[0] = _silu(mm(7)).astype(BF16)


def _proj(x, mod3, nw, w_in, grp, qn, kn, rope_tabs, *, ctx_row):
    b, lx, d = x.shape
    rope = ctx_row is None
    tm = min(TOKEN_TILE, lx)
    n_cols = w_in.shape[1]
    if rope:
        mrow = lambda i, bb: bb
    else:
        mrow = lambda i, bb: ctx_row
    in_specs = [
        pl.BlockSpec((1, tm, d), lambda i, bb: (bb, i, 0)),
        pl.BlockSpec((1, 1, d), lambda i, bb: (mrow(i, bb), 0, 0)),
        pl.BlockSpec((1, 1, d), lambda i, bb: (mrow(i, bb), 0, 1)),
        pl.BlockSpec((1, d), lambda i, bb: (0, 0)),
        pl.BlockSpec((d, n_cols), lambda i, bb: (0, 0)),
        pl.BlockSpec((NA_WIDTH, NA_WIDTH), lambda i, bb: (0, 0)),
        pl.BlockSpec((1, NA_WIDTH), lambda i, bb: (0, 0)),
        pl.BlockSpec((1, NA_WIDTH), lambda i, bb: (0, 0)),
    ]
    args = [x, mod3, mod3, nw, w_in, grp, qn, kn]
    if rope:
        in_specs += [pl.BlockSpec((tm, RET_HEAD_DIM), lambda i, bb: (i, 0))] * 2
        args += list(rope_tabs)
    out_spec = pl.BlockSpec((1, tm, NA_WIDTH), lambda i, bb: (bb, i, 0))
    out_shape = jax.ShapeDtypeStruct((b, lx, NA_WIDTH), BF16)
    return pl.pallas_call(
        functools.partial(_proj_kernel, rope=rope),
        grid=(lx // tm, b),
        in_specs=in_specs,
        out_specs=[out_spec] * 8,
        out_shape=[out_shape] * 8,
        compiler_params=_params(2),
        name="in_proj_lat" if rope else "in_proj_ctx",
    )(*args)


def _attn_kernel(*refs, latent, n_steps):
    if latent:
        q_ref, k_ref, v_ref, kc_ref, vc_ref, bias_ref, o_ref = refs
        i = pl.program_id(1)
        rows = n_steps * NA_Q_ROWS
        ws = jnp.clip(i * NA_Q_ROWS - NA_WIN_H // 2, 0, rows - NA_K_ROWS)
        kstart = pl.multiple_of(ws * GRID_W, GRID_W)
    else:
        q_ref, kc_ref, vc_ref, o_ref = refs
    lane = lax.broadcasted_iota(I32, (1, 2 * NA_HEAD_DIM), 1)
    lo = lane < NA_HEAD_DIM
    for j in range(NA_HEADS // 2):
        cs = slice(j * 2 * NA_HEAD_DIM, (j + 1) * 2 * NA_HEAD_DIM)
        qp = q_ref[0, :, cs]
        kcp = kc_ref[0, :, cs]
        vcp = vc_ref[0, :, cs]
        if latent:
            kp = k_ref[0, pl.ds(kstart, NA_K_ROWS * GRID_W), cs]
            vp = v_ref[0, pl.ds(kstart, NA_K_ROWS * GRID_W), cs]
        halves = []
        for half in range(2):
            qm = jnp.where(lo if half == 0 else jnp.logical_not(lo), qp, jnp.zeros_like(qp))
            s_ctx = lax.dot_general(qm, kcp, _NT, preferred_element_type=F32)
            m = jnp.max(s_ctx, axis=-1, keepdims=True)
            if latent:
                s_lat = lax.dot_general(qm, kp, _NT, preferred_element_type=F32) + bias_ref[0, 2 * j + half]
                m = jnp.maximum(m, jnp.max(s_lat, axis=-1, keepdims=True))
            p_ctx = jnp.exp(s_ctx - m)
            den = jnp.sum(p_ctx, axis=-1, keepdims=True)
            o = jnp.dot(p_ctx.astype(BF16), vcp, preferred_element_type=F32)
            if latent:
                p_lat = jnp.exp(s_lat - m)
                den = den + jnp.sum(p_lat, axis=-1, keepdims=True)
                o = o + jnp.dot(p_lat.astype(BF16), vp, preferred_element_type=F32)
            halves.append(o / den)
        o_ref[0, :, cs] = jnp.where(lo, halves[0], halves[1]).astype(BF16)


def _na_attention(qa, ka, va, kc, vc, bias):
    b, l, w = qa.shape
    lc = kc.shape[1]
    tq = NA_Q_ROWS * GRID_W
    tk = NA_K_ROWS * GRID_W
    n_steps = l // tq
    case = lambda i: jnp.where(i == 0, 0, jnp.where(i == n_steps - 1, 2, 1))
    return pl.pallas_call(
        functools.partial(_attn_kernel, latent=True, n_steps=n_steps),
        grid=(b, n_steps),
        in_specs=[pl.BlockSpec((1, tq, w), lambda bb, i: (bb, i, 0)),
                  pl.BlockSpec((1, l, w), lambda bb, i: (bb, 0, 0)),
                  pl.BlockSpec((1, l, w), lambda bb, i: (bb, 0, 0)),
                  pl.BlockSpec((1, lc, w), lambda bb, i: (bb, 0, 0)),
                  pl.BlockSpec((1, lc, w), lambda bb, i: (bb, 0, 0)),
                  pl.BlockSpec((1, NA_HEADS, tq, tk), lambda bb, i: (case(i), 0, 0, 0))],
        out_specs=pl.BlockSpec((1, tq, w), lambda bb, i: (bb, i, 0)),
        out_shape=jax.ShapeDtypeStruct((b, l, w), BF16),
        compiler_params=_params(2),
        name="na_attention",
    )(qa, ka, va, kc, vc, bias)


def _ctx_attention(qc, kc, vc):
    b, lc, w = qc.shape
    spec = pl.BlockSpec((1, lc, w), lambda bb: (bb, 0, 0))
    return pl.pallas_call(
        functools.partial(_attn_kernel, latent=False, n_steps=1),
        grid=(b,),
        in_specs=[spec, spec, spec],
        out_specs=spec,
        out_shape=jax.ShapeDtypeStruct((b, lc, w), BF16),
        compiler_params=_params(1),
        name="ctx_attention",
    )(qc, kc, vc)


def _na_bias_table(rpb):
    h = rpb.shape[0]
    qcol = np.arange(GRID_W)
    kcol = np.arange(GRID_W)
    c0 = np.clip(qcol - NA_WIN_W // 2, 0, GRID_W - NA_WIN_W)
    col_ok = (kcol[None, :] >= c0[:, None]) & (kcol[None, :] < c0[:, None] + NA_WIN_W)
    dc = np.clip(kcol[None, :] - qcol[:, None] + NA_WIN_W - 1, 0, 2 * NA_WIN_W - 2)
    by_dr = jnp.where(jnp.asarray(col_ok)[None, None], rpb[:, :, jnp.asarray(dc)], NEG_INF)
    masked = jnp.full((h, GRID_W, GRID_W), NEG_INF, F32)
    cases = []
    for lo_fn, dr_off in ((lambda qr: 0, NA_WIN_H - 1),
                          (lambda qr: qr, NA_WIN_H // 2 - 1),
                          (lambda qr: NA_K_ROWS - NA_WIN_H, -1)):
        rows = []
        for qr in range(NA_Q_ROWS):
            blocks = []
            for kr in range(NA_K_ROWS):
                ok = lo_fn(qr) <= kr < lo_fn(qr) + NA_WIN_H
                blocks.append(by_dr[:, kr - qr + dr_off] if ok else masked)
            rows.append(jnp.concatenate(blocks, axis=2))
        cases.append(jnp.concatenate(rows, axis=1))
    return jnp.stack(cases, axis=0)


def _ret_kernel(rd_ref, q_ref, k_ref, v_ref, gf_ref, gb_ref, s0_ref, o_ref, sfin_ref, accf_ref, accb_ref,
                intra_ref, dec_ref, *, c, n_chunks):
    hd = pl.program_id(0)
    log_g_all = -jnp.exp(rd_ref[...])
    head = lax.broadcasted_iota(I32, log_g_all.shape, 1)
    log_g = jnp.sum(jnp.where(head == hd, log_g_all, 0.0), axis=1, keepdims=True)
    lg = (log_g[0:1, :], log_g[1:2, :])

    @pl.when(pl.program_id(1) == 0)
    def _():
        diff = (lax.broadcasted_iota(I32, (c, c), 0) - lax.broadcasted_iota(I32, (c, c), 1)).astype(F32)
        pos = lax.broadcasted_iota(I32, (c, RET_HEAD_DIM), 0).astype(F32)
        intra_ref[0] = jnp.where(diff >= 0, jnp.exp(jnp.maximum(diff, 0.0) * lg[0]), 0.0)
        dec_ref[0] = jnp.exp((pos + 1.0) * lg[0])
        dec_ref[1] = jnp.exp((c - 1.0 - pos) * lg[0])
        intra_ref[1] = jnp.where(diff <= 0, jnp.exp(jnp.maximum(-diff, 0.0) * lg[1]), 0.0)
        dec_ref[2] = jnp.exp((c - pos) * lg[1])
        dec_ref[3] = jnp.exp(pos * lg[1])

    def direction(d, gate_ref, part_ref):
        c_dec = jnp.exp(c * lg[d])

        def step(t, s):
            n = t if d == 0 else n_chunks - 1 - t
            off = pl.multiple_of(n * c, c)
            qb = q_ref[0, pl.ds(off, c), :]
            kb = k_ref[0, pl.ds(off, c), :]
            vb = v_ref[0, pl.ds(off, c), :]
            scores = lax.dot_general(qb, kb, _NT, preferred_element_type=F32) * intra_ref[d]
            o = (jnp.dot(scores.astype(BF16), vb, preferred_element_type=F32)
                 + jnp.dot(qb, s.astype(BF16), preferred_element_type=F32) * dec_ref[2 * d])
            kd = (kb.astype(F32) * dec_ref[2 * d + 1]).astype(BF16)
            s = s * c_dec + lax.dot_general(kd, vb, _TN, preferred_element_type=F32)
            mu = jnp.mean(o, axis=-1, keepdims=True)
            oc = o - mu
            var = jnp.mean(oc * oc, axis=-1, keepdims=True)
            part_ref[pl.ds(off, c), :] = (oc * lax.rsqrt(var + HEAD_NORM_EPS)
                                          * gate_ref[0, pl.ds(off, c), :].astype(F32))
            return s

        return step

    step_f = direction(0, gf_ref, accf_ref)
    step_b = direction(1, gb_ref, accb_ref)

    def both(t, carry):
        return step_f(t, carry[0]), step_b(t, carry[1])

    s_f, s_b = lax.fori_loop(0, n_chunks, both, (s0_ref[0, 0, 0], s0_ref[0, 0, 1]), unroll=min(2, n_chunks))
    sfin_ref[0, 0, 0] = s_f
    sfin_ref[0, 0, 1] = s_b

    def combine(i, carry):
        off = pl.multiple_of(i * c, c)
        o_ref[0, pl.ds(off, c), :] = (accf_ref[pl.ds(off, c), :] + accb_ref[pl.ds(off, c), :]).astype(BF16)
        return carry

    lax.fori_loop(0, n_chunks, combine, 0)


def _retention(ret_decay, q, k, v, gf, gb, s0):
    b, lx, w = q.shape
    hd = RET_HEAD_DIM
    c = min(RET_CHUNK, lx)
    seq = pl.BlockSpec((1, lx, hd), lambda h, bb: (bb, 0, h))
    st = pl.BlockSpec((1, 1, 2, hd, hd), lambda h, bb: (bb, h, 0, 0, 0))
    return pl.pallas_call(
        functools.partial(_ret_kernel, c=c, n_chunks=lx // c),
        grid=(RET_HEADS, b),
        in_specs=[pl.BlockSpec((2, RET_HEADS), lambda h, bb: (0, 0)), seq, seq, seq, seq, seq, st],
        out_specs=[seq, st],
        out_shape=[jax.ShapeDtypeStruct((b, lx, w), BF16),
                   jax.ShapeDtypeStruct((b, RET_HEADS, 2, hd, hd), F32)],
        scratch_shapes=[pltpu.VMEM((lx, hd), F32), pltpu.VMEM((lx, hd), F32), pltpu.VMEM((2, c, c), F32),
                        pltpu.VMEM((4, c, hd), F32)],
        compiler_params=_params(2),
        name="retention",
    )(ret_decay, q, k, v, gf, gb, s0)


def _mix_kernel(x_ref, na_ref, rt_ref, wo_ref, ga_ref, shf_ref, scf_ref, nw_ref, rwt_ref, rb_ref, tri_ref,
                cnt_in_ref, xo_ref, tok_ref, code_ref, gate_ref, cnt_out_ref, cnt_sc):
    @pl.when(pl.program_id(0) == 0)
    def _():
        cnt_sc[...] = cnt_in_ref[...]

    mix = (jnp.dot(na_ref[...], wo_ref[0:NA_WIDTH, :], preferred_element_type=F32)
           + jnp.dot(rt_ref[...], wo_ref[NA_WIDTH:, :], preferred_element_type=F32))
    x = x_ref[...] + ga_ref[0] * mix
    xo_ref[...] = x
    ms = jnp.mean(x * x, axis=-1, keepdims=True)
    t = x * lax.rsqrt(ms + NORM_EPS) * nw_ref[...]
    t = t * (1.0 + scf_ref[0]) + shf_ref[0]
    tok_ref[...] = t

    tm = x.shape[0]
    logits = lax.dot_general(rwt_ref[...], t.astype(BF16), _NT, preferred_element_type=F32)
    s = jax.nn.sigmoid(logits)
    s_sel = s + rb_ref[...]
    a = [s_sel[j * N_GROUPS:(j + 1) * N_GROUPS, :] for j in range(EXPERTS_PER_GROUP)]
    u = [s[j * N_GROUPS:(j + 1) * N_GROUPS, :] for j in range(EXPERTS_PER_GROUP)]
    hi01, lo01 = jnp.maximum(a[0], a[1]), jnp.minimum(a[0], a[1])
    hi23, lo23 = jnp.maximum(a[2], a[3]), jnp.minimum(a[2], a[3])
    gscore = jnp.maximum(hi01, hi23) + jnp.maximum(jnp.minimum(hi01, hi23), jnp.maximum(lo01, lo23))
    gidx = lax.broadcasted_iota(I32, (N_GROUPS, tm), 0)
    gmax = jnp.max(gscore, axis=0, keepdims=True)
    grp = jnp.min(jnp.where(gscore == gmax, gidx.astype(F32), float(N_GROUPS)), axis=0,
                  keepdims=True).astype(I32)
    sel = gidx == grp
    av = [jnp.sum(jnp.where(sel, a[j], 0.0), axis=0, keepdims=True) for j in range(EXPERTS_PER_GROUP)]
    uv = [jnp.sum(jnp.where(sel, u[j], 0.0), axis=0, keepdims=True) for j in range(EXPERTS_PER_GROUP)]
    best1, loc1 = av[0], jnp.zeros((1, tm), I32)
    for j in range(1, EXPERTS_PER_GROUP):
        take = av[j] > best1
        best1 = jnp.where(take, av[j], best1)
        loc1 = jnp.where(take, j, loc1)
    best2, loc2 = jnp.full((1, tm), -jnp.inf, F32), jnp.full((1, tm), -1, I32)
    for j in range(EXPERTS_PER_GROUP):
        take = jnp.logical_and(loc1 != j, jnp.logical_or(loc2 < 0, av[j] > best2))
        best2 = jnp.where(take, av[j], best2)
        loc2 = jnp.where(take, j, loc2)
    g1 = sum(jnp.where(loc1 == j, uv[j], 0.0) for j in range(EXPERTS_PER_GROUP))
    g2 = sum(jnp.where(loc2 == j, uv[j], 0.0) for j in range(EXPERTS_PER_GROUP))
    gate_ref[0:1, :] = g1 / (g1 + g2)
    gate_ref[1:2, :] = g2 / (g1 + g2)

    r1 = loc1 * N_GROUPS + grp
    r2 = loc2 * N_GROUPS + grp
    ridx = lax.broadcasted_iota(I32, (N_EXPERTS, tm), 0)
    hit1 = ridx == r1
    hit2 = ridx == r2
    onehot = jnp.logical_or(hit1, hit2).astype(F32)
    before = jnp.dot(onehot.astype(BF16), tri_ref[...], preferred_element_type=F32) + cnt_sc[...]
    rank1 = jnp.sum(jnp.where(hit1, before, 0.0), axis=0, keepdims=True).astype(I32)
    rank2 = jnp.sum(jnp.where(hit2, before, 0.0), axis=0, keepdims=True).astype(I32)
    code_ref[0:1, :] = (r1 << RANK_BITS) | rank1
    code_ref[1:2, :] = (r2 << RANK_BITS) | rank2
    cnt_sc[...] = cnt_sc[...] + jnp.sum(onehot, axis=1, keepdims=True)
    cnt_out_ref[...] = cnt_sc[...]


def _mix(xf, na, rt, w_out, mod3, nw, rwt, rb, tri, cnt_in, *, tiles_per_row, ctx_row):
    t, d = xf.shape
    tm = TOKEN_TILE
    if ctx_row is None:
        mrow = lambda j: j // tiles_per_row
    else:
        mrow = lambda j: ctx_row
    return pl.pallas_call(
        _mix_kernel,
        grid=(t // tm,),
        in_specs=[pl.BlockSpec((tm, d), lambda j: (j, 0)),
                  pl.BlockSpec((tm, NA_WIDTH), lambda j: (j, 0)),
                  pl.BlockSpec((tm, RET_WIDTH), lambda j: (j, 0)),
                  pl.BlockSpec((NA_WIDTH + RET_WIDTH, d), lambda j: (0, 0)),
                  pl.BlockSpec((1, 1, d), lambda j: (mrow(j), 0, 2)),
                  pl.BlockSpec((1, 1, d), lambda j: (mrow(j), 0, 3)),
                  pl.BlockSpec((1, 1, d), lambda j: (mrow(j), 0, 4)),
                  pl.BlockSpec((1, d), lambda j: (0, 0)),
                  pl.BlockSpec((N_EXPERTS, d), lambda j: (0, 0)),
                  pl.BlockSpec((N_EXPERTS, 1), lambda j: (0, 0)),
                  pl.BlockSpec((tm, tm), lambda j: (0, 0)),
                  pl.BlockSpec((N_EXPERTS, 1), lambda j: (0, 0))],
        out_specs=[pl.BlockSpec((tm, d), lambda j: (j, 0)),
                   pl.BlockSpec((tm, d), lambda j: (j, 0)),
                   pl.BlockSpec((2, tm), lambda j: (0, j)),
                   pl.BlockSpec((2, tm), lambda j: (0, j)),
                   pl.BlockSpec((N_EXPERTS, 1), lambda j: (0, 0))],
        out_shape=[jax.ShapeDtypeStruct((t, d), F32),
                   jax.ShapeDtypeStruct((t, d), F32),
                   jax.ShapeDtypeStruct((2, t), I32),
                   jax.ShapeDtypeStruct((2, t), F32),
                   jax.ShapeDtypeStruct((N_EXPERTS, 1), F32)],
        scratch_shapes=[pltpu.VMEM((N_EXPERTS, 1), F32)],
        compiler_params=_params(1),
        name="out_proj_router",
    )(xf, na, rt, w_out, mod3, mod3, mod3, nw, rwt, rb, tri, cnt_in)


def _rows_kernel(pstart_ref, code_ref, rows_ref):
    code = code_ref[...]
    slot = code >> RANK_BITS
    rows = code & ((1 << RANK_BITS) - 1)
    for e in range(N_EXPERTS):
        rows = rows + jnp.where(slot == e, pstart_ref[e], 0)
    rows_ref[...] = rows


def _sorted_rows(pstart, code):
    t = code.shape[1]
    tp = min(t, 4096)
    return pl.pallas_call(
        _rows_kernel,
        grid_spec=pltpu.PrefetchScalarGridSpec(
            num_scalar_prefetch=1,
            grid=(t // tp,),
            in_specs=[pl.BlockSpec((2, tp), lambda j, ps: (0, j))],
            out_specs=pl.BlockSpec((2, tp), lambda j, ps: (0, j))),
        out_shape=jax.ShapeDtypeStruct((2, t), I32),
        compiler_params=_params(1),
        name="moe_rows",
    )(pstart, code)


SMEM_LANES = 128
SUBLANES = 8


def _row_copies(rows_ref, n_tokens, copy):
    groups = SMEM_LANES // SUBLANES
    for blk in range(n_tokens // SMEM_LANES):
        def issue(i, carry, blk=blk):
            base = pl.multiple_of(i * SUBLANES, SUBLANES)
            for u in range(SUBLANES):
                for k in range(2):
                    copy(k, blk * groups + i, u, rows_ref[k, blk, 0, base + u]).start(priority=k)
            return carry

        lax.fori_loop(0, groups, issue, 0)


def _dispatch_kernel(pad_start_ref, pad_len_ref, nvalid_ref, rows_ref, *refs, n_lat_tiles, n_blocks):
    if n_lat_tiles is None:
        tok_refs, (xs_ref, sem, zero_ref, zero_sem) = refs[:1], refs[1:]
    else:
        tok_refs, (xs_ref, sem, zero_ref, zero_sem) = refs[:2], refs[2:]
    j = pl.program_id(0)
    bm = zero_ref.shape[0]

    @pl.when(j == 0)
    def _():
        zero_ref[...] = jnp.zeros_like(zero_ref)

        def pad_copy(e, i):
            return pltpu.make_async_copy(zero_ref.at[pl.ds(0, 1)],
                                         xs_ref.at[pl.ds(pad_start_ref[e] + i, 1)], zero_sem)

        def per_expert(e, carry):
            n = pad_len_ref[e]
            lax.fori_loop(0, n, lambda i, c: (pad_copy(e, i).start(), c)[1], 0)
            lax.fori_loop(0, n, lambda i, c: (pad_copy(e, i).wait(), c)[1], 0)
            return carry

        lax.fori_loop(0, N_EXPERTS, per_expert, 0)

        def tail_copy(i):
            start = pl.multiple_of((nvalid_ref[0] + i) * bm, bm)
            return pltpu.make_async_copy(zero_ref, xs_ref.at[pl.ds(start, bm)], zero_sem)

        n_tail = n_blocks - nvalid_ref[0]
        lax.fori_loop(0, n_tail, lambda i, c: (tail_copy(i).start(), c)[1], 0)
        lax.fori_loop(0, n_tail, lambda i, c: (tail_copy(i).wait(), c)[1], 0)

    def scatter(tok_ref):
        tm = tok_ref.shape[0] * SUBLANES
        _row_copies(rows_ref, tm, lambda k, g, u, row: pltpu.make_async_copy(
            tok_ref.at[g, pl.ds(u, 1)], xs_ref.at[pl.ds(row, 1)], sem))
        for _ in range(2):
            pltpu.make_async_copy(xs_ref.at[pl.ds(0, tm)], xs_ref.at[pl.ds(0, tm)], sem).wait()

    if n_lat_tiles is None:
        scatter(tok_refs[0])
    else:
        pl.when(j < n_lat_tiles)(lambda: scatter(tok_refs[0]))
        pl.when(j >= n_lat_tiles)(lambda: scatter(tok_refs[1]))


def _dispatch(pad_start, pad_len, nvalid, rows, toks, *, n_rows):
    d = toks[0].shape[1]
    tm = TOKEN_TILE
    bm = MOE_BLOCK_ROWS
    t = sum(tk.shape[0] for tk in toks)
    n_lat_tiles = toks[0].shape[0] // tm if len(toks) == 2 else None
    tok_block = (tm // SUBLANES, SUBLANES, d)
    in_specs = [pl.BlockSpec((2, tm // SMEM_LANES, 1, SMEM_LANES), lambda j, *_: (0, j, 0, 0),
                             memory_space=pltpu.SMEM)]
    if n_lat_tiles is None:
        in_specs.append(pl.BlockSpec(tok_block, lambda j, *_: (j, 0, 0)))
    else:
        in_specs.append(pl.BlockSpec(tok_block, lambda j, *_: (jnp.minimum(j, n_lat_tiles - 1), 0, 0)))
        in_specs.append(pl.BlockSpec(tok_block, lambda j, *_: (jnp.maximum(j - n_lat_tiles, 0), 0, 0)))
    return pl.pallas_call(
        functools.partial(_dispatch_kernel, n_lat_tiles=n_lat_tiles, n_blocks=n_rows // bm),
        grid_spec=pltpu.PrefetchScalarGridSpec(
            num_scalar_prefetch=3,
            grid=(t // tm,),
            in_specs=in_specs,
            out_specs=pl.BlockSpec(memory_space=pl.ANY),
            scratch_shapes=[pltpu.SemaphoreType.DMA(()), pltpu.VMEM((bm, d), F32), pltpu.SemaphoreType.DMA(())]),
        out_shape=jax.ShapeDtypeStruct((n_rows, d), F32),
        compiler_params=_params(1),
        name="moe_dispatch",
    )(pad_start, pad_len, nvalid, rows.reshape(2, t // SMEM_LANES, 1, SMEM_LANES),
      *[tk.reshape(tk.shape[0] // SUBLANES, SUBLANES, d) for tk in toks])


def _ffn_kernel(bexp_ref, nvalid_ref, xs_ref, wg_ref, wu_ref, wd_ref, ys_ref, wg_bf, wu_bf, wd_bf):
    i = pl.program_id(0)
    live = i < nvalid_ref[0]
    new_expert = jnp.logical_or(i == 0, bexp_ref[i] != bexp_ref[jnp.maximum(i - 1, 0)])

    @pl.when(jnp.logical_and(live, new_expert))
    def _():
        wg_bf[...] = wg_ref[0].astype(BF16)
        wu_bf[...] = wu_ref[0].astype(BF16)
        wd_bf[...] = wd_ref[0].astype(BF16)

    @pl.when(live)
    def _():
        xb = xs_ref[...].astype(BF16)
        g = jnp.dot(xb, wg_bf[...], preferred_element_type=F32)
        u = jnp.dot(xb, wu_bf[...], preferred_element_type=F32)
        act = (_silu(g) * u).astype(BF16)
        ys_ref[...] = jnp.dot(act, wd_bf[...], preferred_element_type=F32)

    @pl.when(jnp.logical_not(live))
    def _():
        ys_ref[...] = jnp.zeros_like(ys_ref)


def _ffn(bexp, nvalid, xs, wg, wu, wd):
    p, d = xs.shape
    bm = MOE_BLOCK_ROWS
    ff = wg.shape[2]
    live = lambda i, nv: jnp.minimum(i, nv[0] - 1)
    return pl.pallas_call(
        _ffn_kernel,
        grid_spec=pltpu.PrefetchScalarGridSpec(
            num_scalar_prefetch=2,
            grid=(p // bm,),
            in_specs=[pl.BlockSpec((bm, d), lambda i, be, nv: (live(i, nv), 0)),
                      pl.BlockSpec((1, d, ff), lambda i, be, nv: (be[i], 0, 0)),
                      pl.BlockSpec((1, d, ff), lambda i, be, nv: (be[i], 0, 0)),
                      pl.BlockSpec((1, ff, d), lambda i, be, nv: (be[i], 0, 0))],
            out_specs=pl.BlockSpec((bm, d), lambda i, be, nv: (i, 0)),
            scratch_shapes=[pltpu.VMEM((d, ff), BF16), pltpu.VMEM((d, ff), BF16), pltpu.VMEM((ff, d), BF16)]),
        out_shape=jax.ShapeDtypeStruct((p, d), F32),
        compiler_params=_params(1),
        name="moe_ffn",
    )(bexp, nvalid, xs, wg, wu, wd)


def _combine_kernel(rows_ref, rows_next_ref, x_ref, gate_ref, gf_ref, ys_ref, o_ref, ybuf, sems, *, n_tiles):
    tm = x_ref.shape[0]
    j = pl.program_id(0)
    slot = j % 2

    def gather(src_rows_ref, s):
        _row_copies(src_rows_ref, tm, lambda k, g, u, row: pltpu.make_async_copy(
            ys_ref.at[pl.ds(row, 1)], ybuf.at[s, k, g, pl.ds(u, 1)], sems.at[s]))

    @pl.when(j == 0)
    def _():
        gather(rows_ref, slot)

    @pl.when(j + 1 < n_tiles)
    def _():
        gather(rows_next_ref, 1 - slot)

    gcol = [jnp.transpose(jnp.broadcast_to(gate_ref[k:k + 1, :], (128, tm))) for k in range(2)]
    for k in range(2):
        pltpu.make_async_copy(ys_ref.at[pl.ds(0, tm)], ys_ref.at[pl.ds(0, tm)], sems.at[slot]).wait()
    gf = gf_ref[0]
    for cidx in range(x_ref.shape[1] // 128):
        cs = slice(cidx * 128, (cidx + 1) * 128)
        y0 = ybuf[slot, 0, :, :, cs].reshape(tm, 128)
        y1 = ybuf[slot, 1, :, :, cs].reshape(tm, 128)
        o_ref[:, cs] = x_ref[:, cs] + gf[:, cs] * (gcol[0] * y0 + gcol[1] * y1)


def _combine(rows, xf, gate, mod3, ys, *, tiles_per_row, ctx_row):
    t, d = xf.shape
    tm = COMBINE_TILE
    n_tiles = t // tm
    if ctx_row is None:
        mrow = lambda j: j // tiles_per_row
    else:
        mrow = lambda j: ctx_row
    rows4 = rows.reshape(2, t // SMEM_LANES, 1, SMEM_LANES)
    rows_block = (2, tm // SMEM_LANES, 1, SMEM_LANES)
    return pl.pallas_call(
        functools.partial(_combine_kernel, n_tiles=n_tiles),
        grid=(n_tiles,),
        in_specs=[pl.BlockSpec(rows_block, lambda j: (0, j, 0, 0), memory_space=pltpu.SMEM),
                  pl.BlockSpec(rows_block, lambda j: (0, jnp.minimum(j + 1, n_tiles - 1), 0, 0),
                               memory_space=pltpu.SMEM),
                  pl.BlockSpec((tm, d), lambda j: (j, 0)),
                  pl.BlockSpec((2, tm), lambda j: (0, j)),
                  pl.BlockSpec((1, 1, d), lambda j: (mrow(j), 0, 5)),
                  pl.BlockSpec(memory_space=pl.ANY)],
        out_specs=pl.BlockSpec((tm, d), lambda j: (j, 0)),
        scratch_shapes=[pltpu.VMEM((2, 2, tm // SUBLANES, SUBLANES, d), F32), pltpu.SemaphoreType.DMA((2,))],
        out_shape=jax.ShapeDtypeStruct((t, d), F32),
        compiler_params=_params(1),
        name="moe_combine",
    )(rows4, rows4, xf, gate, mod3, ys)


def _rope_tables(l):
    half = RET_HEAD_DIM // 4
    t = jnp.arange(l)
    rows = (t // GRID_W).astype(F32)
    cols = (t % GRID_W).astype(F32)
    freqs = ROPE_BASE ** (-jnp.arange(half, dtype=F32) / half)
    ang_r = rows[:, None] * freqs
    ang_c = cols[:, None] * freqs
    ang = jnp.concatenate([ang_r, ang_r, ang_c, ang_c], axis=1)
    sign = np.tile(np.concatenate([-np.ones(half), np.ones(half)]), 2).astype(np.float32)
    return jnp.cos(ang), jnp.sin(ang) * sign


def _slot_to_expert():
    r = np.arange(N_EXPERTS)
    return (r % N_GROUPS) * EXPERTS_PER_GROUP + r // N_GROUPS


def kernel(x, c, ctx, c_ctx, ada_w, ada_b, norm_mix_w, norm_ffn_w, w_in, na_q_norm, na_k_norm, na_rpb,
           ret_decay, w_out, router_w, router_bias, exp_w_gate, exp_w_up, exp_w_down):
    b, l, d = x.shape
    lc = ctx.shape[1]
    depth = ada_w.shape[0]
    assert b + 1 <= MOD_ROWS and l % (NA_Q_ROWS * GRID_W) == 0 and l // GRID_W >= NA_K_ROWS
    assert (b * l) % TOKEN_TILE == 0 and (b * lc) % TOKEN_TILE == 0 and l % TOKEN_TILE == 0

    cc = jnp.concatenate([c, c_ctx[None], jnp.zeros((MOD_ROWS - b - 1, d), F32)], axis=0)
    mods = _ada(cc, ada_w, ada_b)

    grp = jnp.asarray(np.kron(np.eye(NA_HEADS), np.full((NA_HEAD_DIM, NA_HEAD_DIM), 1.0 / NA_HEAD_DIM)), BF16)
    rope_tabs = _rope_tables(l)
    slot_exp = _slot_to_expert()
    rwt = router_w.T[slot_exp].astype(BF16)
    rb = router_bias[slot_exp].reshape(N_EXPERTS, 1).astype(F32)
    tri = jnp.asarray(np.triu(np.ones((TOKEN_TILE, TOKEN_TILE), np.float32), 1), BF16)
    bm = MOE_BLOCK_ROWS

    for layer in range(depth):
        last = layer == depth - 1
        mod3 = mods[layer].reshape(MOD_ROWS, 1, 6 * d)
        w_in_l = w_in[layer].astype(BF16)
        w_out_l = w_out[layer].astype(BF16)
        qn = jnp.tile(na_q_norm[layer], NA_HEADS)[None].astype(F32)
        kn = jnp.tile(na_k_norm[layer], NA_HEADS)[None].astype(F32)
        nw_mix = norm_mix_w[layer][None]
        nw_ffn = norm_ffn_w[layer][None]

        pc = _proj(ctx, mod3, nw_mix, w_in_l, grp, qn, kn, None, ctx_row=b)
        pq = _proj(x, mod3, nw_mix, w_in_l, grp, qn, kn, rope_tabs, ctx_row=None)
        qac, kac, vac, qbc, kbc, vbc, gfc, gbc = pc
        qa, ka, va, qb, kb, vb, gfw, gbw = pq

        zero_state = jnp.zeros((b, RET_HEADS, 2, RET_HEAD_DIM, RET_HEAD_DIM), F32)
        ret_c, ctx_state = _retention(ret_decay[layer], qbc, kbc, vbc, gfc, gbc, zero_state)
        ret_l, _ = _retention(ret_decay[layer], qb, kb, vb, gfw, gbw, ctx_state)
        na_l = _na_attention(qa, ka, va, kac, vac, _na_bias_table(na_rpb[layer]))

        cnt0 = jnp.zeros((N_EXPERTS, 1), F32)
        xf, tok_l, code_l, gate_l, cnt = _mix(
            x.reshape(b * l, d), na_l.reshape(b * l, NA_WIDTH), ret_l.reshape(b * l, RET_WIDTH), w_out_l, mod3,
            nw_ffn, rwt, rb, tri, cnt0, tiles_per_row=l // TOKEN_TILE, ctx_row=None)
        n_assign = 2 * b * l
        if not last:
            na_c = _ctx_attention(qac, kac, vac)
            cf, tok_c, code_c, gate_c, cnt = _mix(
                ctx.reshape(b * lc, d), na_c.reshape(b * lc, NA_WIDTH), ret_c.reshape(b * lc, RET_WIDTH), w_out_l,
                mod3, nw_ffn, rwt, rb, tri, cnt, tiles_per_row=1, ctx_row=b)
            n_assign += 2 * b * lc

        counts = cnt[:, 0].astype(I32)
        pcounts = (counts + bm - 1) // bm * bm
        pend = jnp.cumsum(pcounts)
        pstart = (pend - pcounts).astype(I32)
        n_rows = n_assign + N_EXPERTS * bm
        n_blocks = n_rows // bm
        nvalid = (pend[-1] // bm).astype(I32)
        blk = jnp.minimum(jnp.arange(n_blocks, dtype=I32), nvalid - 1) * bm
        blk_slot = jnp.minimum(jnp.sum(blk[:, None] >= pend[None, :], axis=1), N_EXPERTS - 1)
        blk_exp = jnp.asarray(slot_exp, I32)[blk_slot]

        pad_start = pstart + counts
        pad_len = pcounts - counts
        if last:
            rows_l = _sorted_rows(pstart, code_l)
            toks = [tok_l]
        else:
            rows_all = _sorted_rows(pstart, jnp.concatenate([code_l, code_c], axis=1))
            rows_l, rows_c = rows_all[:, :b * l], rows_all[:, b * l:]
            toks = [tok_l, tok_c]
        nvalid = nvalid.reshape(1)
        xs = _dispatch(pad_start, pad_len, nvalid, rows_l if last else rows_all, toks, n_rows=n_rows)
        ys = _ffn(blk_exp, nvalid, xs, exp_w_gate[layer], exp_w_up[layer], exp_w_down[layer])
        x = _combine(rows_l, xf, gate_l, mod3, ys, tiles_per_row=l // COMBINE_TILE, ctx_row=None).reshape(b, l, d)
        if not last:
            ctx = _combine(rows_c, cf, gate_c, mod3, ys, tiles_per_row=1, ctx_row=b).reshape(b, lc, d)
    return x
```

```python
import functools

import numpy as np
import jax
import jax.numpy as jnp
from jax import lax
from jax.experimental import pallas as pl
from jax.experimental.pallas import tpu as pltpu

F32 = jnp.float32
BF16 = jnp.bfloat16
I32 = jnp.int32

GRID_W = 64
NA_HEAD_DIM = 64
NA_HEADS = 8
NA_WIDTH = NA_HEADS * NA_HEAD_DIM
NA_WIN_H = 8
NA_WIN_W = 16
NA_Q_ROWS = 4
NA_K_ROWS = NA_Q_ROWS + NA_WIN_H
RET_HEAD_DIM = 128
RET_HEADS = 4
RET_WIDTH = RET_HEADS * RET_HEAD_DIM
RET_CHUNK = 512
ROPE_BASE = 10000.0
N_EXPERTS = 32
N_GROUPS = 8
EXPERTS_PER_GROUP = N_EXPERTS // N_GROUPS
NORM_EPS = 1e-6
HEAD_NORM_EPS = 1e-5
NEG_INF = -1e30
MOE_BLOCK_ROWS = 512
RANK_BITS = 20
TOKEN_TILE = 512
COMBINE_TILE = 256
MOD_ROWS = 24
V7X_VMEM_LIMIT_BYTES = 56 * 1024 * 1024

_NT = (((1,), (1,)), ((), ()))
_TN = (((0,), (0,)), ((), ()))


def _params(n_axes):
    return pltpu.CompilerParams(dimension_semantics=("arbitrary",) * n_axes,
                                vmem_limit_bytes=V7X_VMEM_LIMIT_BYTES)


def _silu(v):
    return v * jax.nn.sigmoid(v)


def _ada_kernel(cc_ref, w_ref, b_ref, o_ref):
    a = _silu(cc_ref[...])
    o_ref[0] = jnp.dot(a, w_ref[0], preferred_element_type=F32, precision=lax.Precision.HIGHEST) + b_ref[0]


def _ada(cc, ada_w, ada_b):
    depth, d, n = ada_w.shape
    tn = 1536
    return pl.pallas_call(
        _ada_kernel,
        grid=(depth, n // tn),
        in_specs=[pl.BlockSpec((MOD_ROWS, d), lambda l, j: (0, 0)),
                  pl.BlockSpec((1, d, tn), lambda l, j: (l, 0, j)),
                  pl.BlockSpec((1, 1, tn), lambda l, j: (l, 0, j))],
        out_specs=pl.BlockSpec((1, MOD_ROWS, tn), lambda l, j: (l, 0, j)),
        out_shape=jax.ShapeDtypeStruct((depth, MOD_ROWS, n), F32),
        compiler_params=_params(2),
        name="ada_mod",
    )(cc, ada_w, ada_b.reshape(depth, 1, n))


def _proj_kernel(*refs, rope):
    if rope:
        (x_ref, sh_ref, sc_ref, nw_ref, w_ref, grp_ref, qn_ref, kn_ref, cos_ref, sin_ref,
         qa_ref, ka_ref, va_ref, qb_ref, kb_ref, vb_ref, gf_ref, gb_ref) = refs
    else:
        (x_ref, sh_ref, sc_ref, nw_ref, w_ref, grp_ref, qn_ref, kn_ref,
         qa_ref, ka_ref, va_ref, qb_ref, kb_ref, vb_ref, gf_ref, gb_ref) = refs
    x = x_ref[0]
    ms = jnp.mean(x * x, axis=-1, keepdims=True)
    h = x * lax.rsqrt(ms + NORM_EPS) * nw_ref[...]
    h = (h * (1.0 + sc_ref[0]) + sh_ref[0]).astype(BF16)

    def mm(g):
        return jnp.dot(h, w_ref[:, g * NA_WIDTH:(g + 1) * NA_WIDTH], preferred_element_type=F32)

    def head_rms(a, wn):
        ss = jnp.dot((a * a).astype(BF16), grp_ref[...], preferred_element_type=F32)
        return a * lax.rsqrt(ss + NORM_EPS) * wn

    def rope_fn(a):
        if not rope:
            return a
        lane = lax.broadcasted_iota(I32, (1, RET_HEAD_DIM), 1)
        first = (lane % 64) < 32
        cos = cos_ref[...]
        sin = sin_ref[...]
        parts = []
        for hh in range(RET_HEADS):
            ah = a[:, hh * RET_HEAD_DIM:(hh + 1) * RET_HEAD_DIM]
            rot = jnp.where(first, pltpu.roll(ah, RET_HEAD_DIM - 32, 1), pltpu.roll(ah, 32, 1))
            parts.append(ah * cos + rot * sin)
        return jnp.concatenate(parts, axis=1)

    qa_ref[0] = (head_rms(mm(0), qn_ref[...]) * (NA_HEAD_DIM ** -0.5)).astype(BF16)
    ka_ref[0] = head_rms(mm(1), kn_ref[...]).astype(BF16)
    va_ref[0] = mm(2).astype(BF16)
    qb_ref[0] = rope_fn(mm(3)).astype(BF16)
    kb_ref[0] = (rope_fn(mm(4)) * (RET_HEAD_DIM ** -0.5)).astype(BF16)
    vb_ref[0] = mm(5).astype(BF16)
    gf_ref[0] = _silu(mm(6)).astype(BF16)
    gb_ref[0] = _silu(mm(7)).astype(BF16)


def _proj(x, mod3, nw, w_in, grp, qn, kn, rope_tabs, *, ctx_row):
    b, lx, d = x.shape
    rope = ctx_row is None
    tm = min(TOKEN_TILE, lx)
    n_cols = w_in.shape[1]
    if rope:
        mrow = lambda i, bb: bb
    else:
        mrow = lambda i, bb: ctx_row
    in_specs = [
        pl.BlockSpec((1, tm, d), lambda i, bb: (bb, i, 0)),
        pl.BlockSpec((1, 1, d), lambda i, bb: (mrow(i, bb), 0, 0)),
        pl.BlockSpec((1, 1, d), lambda i, bb: (mrow(i, bb), 0, 1)),
        pl.BlockSpec((1, d), lambda i, bb: (0, 0)),
        pl.BlockSpec((d, n_cols), lambda i, bb: (0, 0)),
        pl.BlockSpec((NA_WIDTH, NA_WIDTH), lambda i, bb: (0, 0)),
        pl.BlockSpec((1, NA_WIDTH), lambda i, bb: (0, 0)),
        pl.BlockSpec((1, NA_WIDTH), lambda i, bb: (0, 0)),
    ]
    args = [x, mod3, mod3, nw, w_in, grp, qn, kn]
    if rope:
        in_specs += [pl.BlockSpec((tm, RET_HEAD_DIM), lambda i, bb: (i, 0))] * 2
        args += list(rope_tabs)
    out_spec = pl.BlockSpec((1, tm, NA_WIDTH), lambda i, bb: (bb, i, 0))
    out_shape = jax.ShapeDtypeStruct((b, lx, NA_WIDTH), BF16)
    return pl.pallas_call(
        functools.partial(_proj_kernel, rope=rope),
        grid=(lx // tm, b),
        in_specs=in_specs,
        out_specs=[out_spec] * 8,
        out_shape=[out_shape] * 8,
        compiler_params=_params(2),
        name="in_proj_lat" if rope else "in_proj_ctx",
    )(*args)


def _attn_kernel(*refs, latent, n_steps):
    if latent:
        q_ref, k_ref, v_ref, kc_ref, vc_ref, bias_ref, o_ref = refs
        i = pl.program_id(1)
        rows = n_steps * NA_Q_ROWS
        ws = jnp.clip(i * NA_Q_ROWS - NA_WIN_H // 2, 0, rows - NA_K_ROWS)
        kstart = pl.multiple_of(ws * GRID_W, GRID_W)
    else:
        q_ref, kc_ref, vc_ref, o_ref = refs
    lane = lax.broadcasted_iota(I32, (1, 2 * NA_HEAD_DIM), 1)
    lo = lane < NA_HEAD_DIM
    for j in range(NA_HEADS // 2):
        cs = slice(j * 2 * NA_HEAD_DIM, (j + 1) * 2 * NA_HEAD_DIM)
        qp = q_ref[0, :, cs]
        kcp = kc_ref[0, :, cs]
        vcp = vc_ref[0, :, cs]
        if latent:
            kp = k_ref[0, pl.ds(kstart, NA_K_ROWS * GRID_W), cs]
            vp = v_ref[0, pl.ds(kstart, NA_K_ROWS * GRID_W), cs]
        halves = []
        for half in range(2):
            qm = jnp.where(lo if half == 0 else jnp.logical_not(lo), qp, jnp.zeros_like(qp))
            s_ctx = lax.dot_general(qm, kcp, _NT, preferred_element_type=F32)
            m = jnp.max(s_ctx, axis=-1, keepdims=True)
            if latent:
                s_lat = lax.dot_general(qm, kp, _NT, preferred_element_type=F32) + bias_ref[0, 2 * j + half]
                m = jnp.maximum(m, jnp.max(s_lat, axis=-1, keepdims=True))
            p_ctx = jnp.exp(s_ctx - m)
            den = jnp.sum(p_ctx, axis=-1, keepdims=True)
            o = jnp.dot(p_ctx.astype(BF16), vcp, preferred_element_type=F32)
            if latent:
                p_lat = jnp.exp(s_lat - m)
                den = den + jnp.sum(p_lat, axis=-1, keepdims=True)
                o = o + jnp.dot(p_lat.astype(BF16), vp, preferred_element_type=F32)
            halves.append(o / den)
        o_ref[0, :, cs] = jnp.where(lo, halves[0], halves[1]).astype(BF16)


def _na_attention(qa, ka, va, kc, vc, bias):
    b, l, w = qa.shape
    lc = kc.shape[1]
    tq = NA_Q_ROWS * GRID_W
    tk = NA_K_ROWS * GRID_W
    n_steps = l // tq
    case = lambda i: jnp.where(i == 0, 0, jnp.where(i == n_steps - 1, 2, 1))
    return pl.pallas_call(
        functools.partial(_attn_kernel, latent=True, n_steps=n_steps),
        grid=(b, n_steps),
        in_specs=[pl.BlockSpec((1, tq, w), lambda bb, i: (bb, i, 0)),
                  pl.BlockSpec((1, l, w), lambda bb, i: (bb, 0, 0)),
                  pl.BlockSpec((1, l, w), lambda bb, i: (bb, 0, 0)),
                  pl.BlockSpec((1, lc, w), lambda bb, i: (bb, 0, 0)),
                  pl.BlockSpec((1, lc, w), lambda bb, i: (bb, 0, 0)),
                  pl.BlockSpec((1, NA_HEADS, tq, tk), lambda bb, i: (case(i), 0, 0, 0))],
        out_specs=pl.BlockSpec((1, tq, w), lambda bb, i: (bb, i, 0)),
        out_shape=jax.ShapeDtypeStruct((b, l, w), BF16),
        compiler_params=_params(2),
        name="na_attention",
    )(qa, ka, va, kc, vc, bias)


def _ctx_attention(qc, kc, vc):
    b, lc, w = qc.shape
    spec = pl.BlockSpec((1, lc, w), lambda bb: (bb, 0, 0))
    return pl.pallas_call(
        functools.partial(_attn_kernel, latent=False, n_steps=1),
        grid=(b,),
        in_specs=[spec, spec, spec],
        out_specs=spec,
        out_shape=jax.ShapeDtypeStruct((b, lc, w), BF16),
        compiler_params=_params(1),
        name="ctx_attention",
    )(qc, kc, vc)


def _na_bias_table(rpb):
    h = rpb.shape[0]
    qcol = np.arange(GRID_W)
    kcol = np.arange(GRID_W)
    c0 = np.clip(qcol - NA_WIN_W // 2, 0, GRID_W - NA_WIN_W)
    col_ok = (kcol[None, :] >= c0[:, None]) & (kcol[None, :] < c0[:, None] + NA_WIN_W)
    dc = np.clip(kcol[None, :] - qcol[:, None] + NA_WIN_W - 1, 0, 2 * NA_WIN_W - 2)
    by_dr = jnp.where(jnp.asarray(col_ok)[None, None], rpb[:, :, jnp.asarray(dc)], NEG_INF)
    masked = jnp.full((h, GRID_W, GRID_W), NEG_INF, F32)
    cases = []
    for lo_fn, dr_off in ((lambda qr: 0, NA_WIN_H - 1),
                          (lambda qr: qr, NA_WIN_H // 2 - 1),
                          (lambda qr: NA_K_ROWS - NA_WIN_H, -1)):
        rows = []
        for qr in range(NA_Q_ROWS):
            blocks = []
            for kr in range(NA_K_ROWS):
                ok = lo_fn(qr) <= kr < lo_fn(qr) + NA_WIN_H
                blocks.append(by_dr[:, kr - qr + dr_off] if ok else masked)
            rows.append(jnp.concatenate(blocks, axis=2))
        cases.append(jnp.concatenate(rows, axis=1))
    return jnp.stack(cases, axis=0)


def _ret_kernel(rd_ref, q_ref, k_ref, v_ref, gf_ref, gb_ref, s0_ref, o_ref, sfin_ref, accf_ref, accb_ref,
                intra_ref, dec_ref, *, c, n_chunks):
    hd = pl.program_id(0)
    log_g_all = -jnp.exp(rd_ref[...])
    head = lax.broadcasted_iota(I32, log_g_all.shape, 1)
    log_g = jnp.sum(jnp.where(head == hd, log_g_all, 0.0), axis=1, keepdims=True)
    lg = (log_g[0:1, :], log_g[1:2, :])

    @pl.when(pl.program_id(1) == 0)
    def _():
        diff = (lax.broadcasted_iota(I32, (c, c), 0) - lax.broadcasted_iota(I32, (c, c), 1)).astype(F32)
        pos = lax.broadcasted_iota(I32, (c, RET_HEAD_DIM), 0).astype(F32)
        intra_ref[0] = jnp.where(diff >= 0, jnp.exp(jnp.maximum(diff, 0.0) * lg[0]), 0.0)
        dec_ref[0] = jnp.exp((pos + 1.0) * lg[0])
        dec_ref[1] = jnp.exp((c - 1.0 - pos) * lg[0])
        intra_ref[1] = jnp.where(diff <= 0, jnp.exp(jnp.maximum(-diff, 0.0) * lg[1]), 0.0)
        dec_ref[2] = jnp.exp((c - pos) * lg[1])
        dec_ref[3] = jnp.exp(pos * lg[1])

    def direction(d, gate_ref, part_ref):
        c_dec = jnp.exp(c * lg[d])

        def step(t, s):
            n = t if d == 0 else n_chunks - 1 - t
            off = pl.multiple_of(n * c, c)
            qb = q_ref[0, pl.ds(off, c), :]
            kb = k_ref[0, pl.ds(off, c), :]
            vb = v_ref[0, pl.ds(off, c), :]
            scores = lax.dot_general(qb, kb, _NT, preferred_element_type=F32) * intra_ref[d]
            o = (jnp.dot(scores.astype(BF16), vb, preferred_element_type=F32)
                 + jnp.dot(qb, s.astype(BF16), preferred_element_type=F32) * dec_ref[2 * d])
            kd = (kb.astype(F32) * dec_ref[2 * d + 1]).astype(BF16)
            s = s * c_dec + lax.dot_general(kd, vb, _TN, preferred_element_type=F32)
            mu = jnp.mean(o, axis=-1, keepdims=True)
            oc = o - mu
            var = jnp.mean(oc * oc, axis=-1, keepdims=True)
            part_ref[pl.ds(off, c), :] = (oc * lax.rsqrt(var + HEAD_NORM_EPS)
                                          * gate_ref[0, pl.ds(off, c), :].astype(F32))
            return s

        return step

    step_f = direction(0, gf_ref, accf_ref)
    step_b = direction(1, gb_ref, accb_ref)

    def both(t, carry):
        return step_f(t, carry[0]), step_b(t, carry[1])

    s_f, s_b = lax.fori_loop(0, n_chunks, both, (s0_ref[0, 0, 0], s0_ref[0, 0, 1]), unroll=min(2, n_chunks))
    sfin_ref[0, 0, 0] = s_f
    sfin_ref[0, 0, 1] = s_b

    def combine(i, carry):
        off = pl.multiple_of(i * c, c)
        o_ref[0, pl.ds(off, c), :] = (accf_ref[pl.ds(off, c), :] + accb_ref[pl.ds(off, c), :]).astype(BF16)
        return carry

    lax.fori_loop(0, n_chunks, combine, 0)


def _retention(ret_decay, q, k, v, gf, gb, s0):
    b, lx, w = q.shape
    hd = RET_HEAD_DIM
    c = min(RET_CHUNK, lx)
    seq = pl.BlockSpec((1, lx, hd), lambda h, bb: (bb, 0, h))
    st = pl.BlockSpec((1, 1, 2, hd, hd), lambda h, bb: (bb, h, 0, 0, 0))
    return pl.pallas_call(
        functools.partial(_ret_kernel, c=c, n_chunks=lx // c),
        grid=(RET_HEADS, b),
        in_specs=[pl.BlockSpec((2, RET_HEADS), lambda h, bb: (0, 0)), seq, seq, seq, seq, seq, st],
        out_specs=[seq, st],
        out_shape=[jax.ShapeDtypeStruct((b, lx, w), BF16),
                   jax.ShapeDtypeStruct((b, RET_HEADS, 2, hd, hd), F32)],
        scratch_shapes=[pltpu.VMEM((lx, hd), F32), pltpu.VMEM((lx, hd), F32), pltpu.VMEM((2, c, c), F32),
                        pltpu.VMEM((4, c, hd), F32)],
        compiler_params=_params(2),
        name="retention",
    )(ret_decay, q, k, v, gf, gb, s0)


def _mix_kernel(x_ref, na_ref, rt_ref, wo_ref, ga_ref, shf_ref, scf_ref, nw_ref, rwt_ref, rb_ref, tri_ref,
                cnt_in_ref, xo_ref, tok_ref, code_ref, gate_ref, cnt_out_ref, cnt_sc):
    @pl.when(pl.program_id(0) == 0)
    def _():
        cnt_sc[...] = cnt_in_ref[...]

    mix = (jnp.dot(na_ref[...], wo_ref[0:NA_WIDTH, :], preferred_element_type=F32)
           + jnp.dot(rt_ref[...], wo_ref[NA_WIDTH:, :], preferred_element_type=F32))
    x = x_ref[...] + ga_ref[0] * mix
    xo_ref[...] = x
    ms = jnp.mean(x * x, axis=-1, keepdims=True)
    t = x * lax.rsqrt(ms + NORM_EPS) * nw_ref[...]
    t = t * (1.0 + scf_ref[0]) + shf_ref[0]
    tok_ref[...] = t

    tm = x.shape[0]
    logits = lax.dot_general(rwt_ref[...], t.astype(BF16), _NT, preferred_element_type=F32)
    s = jax.nn.sigmoid(logits)
    s_sel = s + rb_ref[...]
    a = [s_sel[j * N_GROUPS:(j + 1) * N_GROUPS, :] for j in range(EXPERTS_PER_GROUP)]
    u = [s[j * N_GROUPS:(j + 1) * N_GROUPS, :] for j in range(EXPERTS_PER_GROUP)]
    hi01, lo01 = jnp.maximum(a[0], a[1]), jnp.minimum(a[0], a[1])
    hi23, lo23 = jnp.maximum(a[2], a[3]), jnp.minimum(a[2], a[3])
    gscore = jnp.maximum(hi01, hi23) + jnp.maximum(jnp.minimum(hi01, hi23), jnp.maximum(lo01, lo23))
    gidx = lax.broadcasted_iota(I32, (N_GROUPS, tm), 0)
    gmax = jnp.max(gscore, axis=0, keepdims=True)
    grp = jnp.min(jnp.where(gscore == gmax, gidx.astype(F32), float(N_GROUPS)), axis=0,
                  keepdims=True).astype(I32)
    sel = gidx == grp
    av = [jnp.sum(jnp.where(sel, a[j], 0.0), axis=0, keepdims=True) for j in range(EXPERTS_PER_GROUP)]
    uv = [jnp.sum(jnp.where(sel, u[j], 0.0), axis=0, keepdims=True) for j in range(EXPERTS_PER_GROUP)]
    best1, loc1 = av[0], jnp.zeros((1, tm), I32)
    for j in range(1, EXPERTS_PER_GROUP):
        take = av[j] > best1
        best1 = jnp.where(take, av[j], best1)
        loc1 = jnp.where(take, j, loc1)
    best2, loc2 = jnp.full((1, tm), -jnp.inf, F32), jnp.full((1, tm), -1, I32)
    for j in range(EXPERTS_PER_GROUP):
        take = jnp.logical_and(loc1 != j, jnp.logical_or(loc2 < 0, av[j] > best2))
        best2 = jnp.where(take, av[j], best2)
        loc2 = jnp.where(take, j, loc2)
    g1 = sum(jnp.where(loc1 == j, uv[j], 0.0) for j in range(EXPERTS_PER_GROUP))
    g2 = sum(jnp.where(loc2 == j, uv[j], 0.0) for j in range(EXPERTS_PER_GROUP))
    gate_ref[0:1, :] = g1 / (g1 + g2)
    gate_ref[1:2, :] = g2 / (g1 + g2)

    r1 = loc1 * N_GROUPS + grp
    r2 = loc2 * N_GROUPS + grp
    ridx = lax.broadcasted_iota(I32, (N_EXPERTS, tm), 0)
    hit1 = ridx == r1
    hit2 = ridx == r2
    onehot = jnp.logical_or(hit1, hit2).astype(F32)
    before = jnp.dot(onehot.astype(BF16), tri_ref[...], preferred_element_type=F32) + cnt_sc[...]
    rank1 = jnp.sum(jnp.where(hit1, before, 0.0), axis=0, keepdims=True).astype(I32)
    rank2 = jnp.sum(jnp.where(hit2, before, 0.0), axis=0, keepdims=True).astype(I32)
    code_ref[0:1, :] = (r1 << RANK_BITS) | rank1
    code_ref[1:2, :] = (r2 << RANK_BITS) | rank2
    cnt_sc[...] = cnt_sc[...] + jnp.sum(onehot, axis=1, keepdims=True)
    cnt_out_ref[...] = cnt_sc[...]


def _mix(xf, na, rt, w_out, mod3, nw, rwt, rb, tri, cnt_in, *, tiles_per_row, ctx_row):
    t, d = xf.shape
    tm = TOKEN_TILE
    if ctx_row is None:
        mrow = lambda j: j // tiles_per_row
    else:
        mrow = lambda j: ctx_row
    return pl.pallas_call(
        _mix_kernel,
        grid=(t // tm,),
        in_specs=[pl.BlockSpec((tm, d), lambda j: (j, 0)),
                  pl.BlockSpec((tm, NA_WIDTH), lambda j: (j, 0)),
                  pl.BlockSpec((tm, RET_WIDTH), lambda j: (j, 0)),
                  pl.BlockSpec((NA_WIDTH + RET_WIDTH, d), lambda j: (0, 0)),
                  pl.BlockSpec((1, 1, d), lambda j: (mrow(j), 0, 2)),
                  pl.BlockSpec((1, 1, d), lambda j: (mrow(j), 0, 3)),
                  pl.BlockSpec((1, 1, d), lambda j: (mrow(j), 0, 4)),
                  pl.BlockSpec((1, d), lambda j: (0, 0)),
                  pl.BlockSpec((N_EXPERTS, d), lambda j: (0, 0)),
                  pl.BlockSpec((N_EXPERTS, 1), lambda j: (0, 0)),
                  pl.BlockSpec((tm, tm), lambda j: (0, 0)),
                  pl.BlockSpec((N_EXPERTS, 1), lambda j: (0, 0))],
        out_specs=[pl.BlockSpec((tm, d), lambda j: (j, 0)),
                   pl.BlockSpec((tm, d), lambda j: (j, 0)),
                   pl.BlockSpec((2, tm), lambda j: (0, j)),
                   pl.BlockSpec((2, tm), lambda j: (0, j)),
                   pl.BlockSpec((N_EXPERTS, 1), lambda j: (0, 0))],
        out_shape=[jax.ShapeDtypeStruct((t, d), F32),
                   jax.ShapeDtypeStruct((t, d), F32),
                   jax.ShapeDtypeStruct((2, t), I32),
                   jax.ShapeDtypeStruct((2, t), F32),
                   jax.ShapeDtypeStruct((N_EXPERTS, 1), F32)],
        scratch_shapes=[pltpu.VMEM((N_EXPERTS, 1), F32)],
        compiler_params=_params(1),
        name="out_proj_router",
    )(xf, na, rt, w_out, mod3, mod3, mod3, nw, rwt, rb, tri, cnt_in)


def _rows_kernel(pstart_ref, code_ref, rows_ref):
    code = code_ref[...]
    slot = code >> RANK_BITS
    rows = code & ((1 << RANK_BITS) - 1)
    for e in range(N_EXPERTS):
        rows = rows + jnp.where(slot == e, pstart_ref[e], 0)
    rows_ref[...] = rows


def _sorted_rows(pstart, code):
    t = code.shape[1]
    tp = min(t, 4096)
    return pl.pallas_call(
        _rows_kernel,
        grid_spec=pltpu.PrefetchScalarGridSpec(
            num_scalar_prefetch=1,
            grid=(t // tp,),
            in_specs=[pl.BlockSpec((2, tp), lambda j, ps: (0, j))],
            out_specs=pl.BlockSpec((2, tp), lambda j, ps: (0, j))),
        out_shape=jax.ShapeDtypeStruct((2, t), I32),
        compiler_params=_params(1),
        name="moe_rows",
    )(pstart, code)


SMEM_LANES = 128
SUBLANES = 8


def _row_copies(rows_ref, n_tokens, copy):
    groups = SMEM_LANES // SUBLANES
    for blk in range(n_tokens // SMEM_LANES):
        def issue(i, carry, blk=blk):
            base = pl.multiple_of(i * SUBLANES, SUBLANES)
            for u in range(SUBLANES):
                for k in range(2):
                    copy(k, blk * groups + i, u, rows_ref[k, blk, 0, base + u]).start(priority=k)
            return carry

        lax.fori_loop(0, groups, issue, 0)


def _dispatch_kernel(pad_start_ref, pad_len_ref, nvalid_ref, rows_ref, *refs, n_lat_tiles, n_blocks):
    if n_lat_tiles is None:
        tok_refs, (xs_ref, sem, zero_ref, zero_sem) = refs[:1], refs[1:]
    else:
        tok_refs, (xs_ref, sem, zero_ref, zero_sem) = refs[:2], refs[2:]
    j = pl.program_id(0)
    bm = zero_ref.shape[0]

    @pl.when(j == 0)
    def _():
        zero_ref[...] = jnp.zeros_like(zero_ref)

        def pad_copy(e, i):
            return pltpu.make_async_copy(zero_ref.at[pl.ds(0, 1)],
                                         xs_ref.at[pl.ds(pad_start_ref[e] + i, 1)], zero_sem)

        def start_pads(e, carry):
            lax.fori_loop(0, pad_len_ref[e], lambda i, c: (pad_copy(e, i).start(), c)[1], 0)
            return carry

        def wait_pads(e, carry):
            lax.fori_loop(0, pad_len_ref[e], lambda i, c: (pad_copy(e, i).wait(), c)[1], 0)
            return carry

        lax.fori_loop(0, N_EXPERTS, start_pads, 0)
        lax.fori_loop(0, N_EXPERTS, wait_pads, 0)

        def tail_copy(i):
            start = pl.multiple_of((nvalid_ref[0] + i) * bm, bm)
            return pltpu.make_async_copy(zero_ref, xs_ref.at[pl.ds(start, bm)], zero_sem)

        n_tail = n_blocks - nvalid_ref[0]
        lax.fori_loop(0, n_tail, lambda i, c: (tail_copy(i).start(), c)[1], 0)
        lax.fori_loop(0, n_tail, lambda i, c: (tail_copy(i).wait(), c)[1], 0)

    def scatter(tok_ref):
        tm = tok_ref.shape[0] * SUBLANES
        _row_copies(rows_ref, tm, lambda k, g, u, row: pltpu.make_async_copy(
            tok_ref.at[g, pl.ds(u, 1)], xs_ref.at[pl.ds(row, 1)], sem))
        for _ in range(2):
            pltpu.make_async_copy(xs_ref.at[pl.ds(0, tm)], xs_ref.at[pl.ds(0, tm)], sem).wait()

    if n_lat_tiles is None:
        scatter(tok_refs[0])
    else:
        pl.when(j < n_lat_tiles)(lambda: scatter(tok_refs[0]))
        pl.when(j >= n_lat_tiles)(lambda: scatter(tok_refs[1]))


def _dispatch(pad_start, pad_len, nvalid, rows, toks, *, n_rows):
    d = toks[0].shape[1]
    tm = TOKEN_TILE
    bm = MOE_BLOCK_ROWS
    t = sum(tk.shape[0] for tk in toks)
    n_lat_tiles = toks[0].shape[0] // tm if len(toks) == 2 else None
    tok_block = (tm // SUBLANES, SUBLANES, d)
    in_specs = [pl.BlockSpec((2, tm // SMEM_LANES, 1, SMEM_LANES), lambda j, *_: (0, j, 0, 0),
                             memory_space=pltpu.SMEM)]
    if n_lat_tiles is None:
        in_specs.append(pl.BlockSpec(tok_block, lambda j, *_: (j, 0, 0)))
    else:
        in_specs.append(pl.BlockSpec(tok_block, lambda j, *_: (jnp.minimum(j, n_lat_tiles - 1), 0, 0)))
        in_specs.append(pl.BlockSpec(tok_block, lambda j, *_: (jnp.maximum(j - n_lat_tiles, 0), 0, 0)))
    return pl.pallas_call(
        functools.partial(_dispatch_kernel, n_lat_tiles=n_lat_tiles, n_blocks=n_rows // bm),
        grid_spec=pltpu.PrefetchScalarGridSpec(
            num_scalar_prefetch=3,
            grid=(t // tm,),
            in_specs=in_specs,
            out_specs=pl.BlockSpec(memory_space=pl.ANY),
            scratch_shapes=[pltpu.SemaphoreType.DMA(()), pltpu.VMEM((bm, d), F32), pltpu.SemaphoreType.DMA(())]),
        out_shape=jax.ShapeDtypeStruct((n_rows, d), F32),
        compiler_params=_params(1),
        name="moe_dispatch",
    )(pad_start, pad_len, nvalid, rows.reshape(2, t // SMEM_LANES, 1, SMEM_LANES),
      *[tk.reshape(tk.shape[0] // SUBLANES, SUBLANES, d) for tk in toks])


def _ffn_kernel(bexp_ref, nvalid_ref, xs_ref, wg_ref, wu_ref, wd_ref, ys_ref, wg_bf, wu_bf, wd_bf):
    i = pl.program_id(0)
    live = i < nvalid_ref[0]
    new_expert = jnp.logical_or(i == 0, bexp_ref[i] != bexp_ref[jnp.maximum(i - 1, 0)])

    @pl.when(jnp.logical_and(live, new_expert))
    def _():
        wg_bf[...] = wg_ref[0].astype(BF16)
        wu_bf[...] = wu_ref[0].astype(BF16)
        wd_bf[...] = wd_ref[0].astype(BF16)

    @pl.when(live)
    def _():
        xb = xs_ref[...].astype(BF16)
        g = jnp.dot(xb, wg_bf[...], preferred_element_type=F32)
        u = jnp.dot(xb, wu_bf[...], preferred_element_type=F32)
        act = (_silu(g) * u).astype(BF16)
        ys_ref[...] = jnp.dot(act, wd_bf[...], preferred_element_type=F32)

    @pl.when(jnp.logical_not(live))
    def _():
        ys_ref[...] = jnp.zeros_like(ys_ref)


def _ffn(bexp, nvalid, xs, wg, wu, wd):
    p, d = xs.shape
    bm = MOE_BLOCK_ROWS
    ff = wg.shape[2]
    live = lambda i, nv: jnp.minimum(i, nv[0] - 1)
    return pl.pallas_call(
        _ffn_kernel,
        grid_spec=pltpu.PrefetchScalarGridSpec(
            num_scalar_prefetch=2,
            grid=(p // bm,),
            in_specs=[pl.BlockSpec((bm, d), lambda i, be, nv: (live(i, nv), 0)),
                      pl.BlockSpec((1, d, ff), lambda i, be, nv: (be[i], 0, 0)),
                      pl.BlockSpec((1, d, ff), lambda i, be, nv: (be[i], 0, 0)),
                      pl.BlockSpec((1, ff, d), lambda i, be, nv: (be[i], 0, 0))],
            out_specs=pl.BlockSpec((bm, d), lambda i, be, nv: (i, 0)),
            scratch_shapes=[pltpu.VMEM((d, ff), BF16), pltpu.VMEM((d, ff), BF16), pltpu.VMEM((ff, d), BF16)]),
        out_shape=jax.ShapeDtypeStruct((p, d), F32),
        compiler_params=_params(1),
        name="moe_ffn",
    )(bexp, nvalid, xs, wg, wu, wd)


def _combine_kernel(rows_ref, rows_next_ref, x_ref, gate_ref, gf_ref, ys_ref, o_ref, ybuf, sems, *, n_tiles):
    tm = x_ref.shape[0]
    j = pl.program_id(0)
    slot = j % 2

    def gather(src_rows_ref, s):
        _row_copies(src_rows_ref, tm, lambda k, g, u, row: pltpu.make_async_copy(
            ys_ref.at[pl.ds(row, 1)], ybuf.at[s, k, g, pl.ds(u, 1)], sems.at[s]))

    @pl.when(j == 0)
    def _():
        gather(rows_ref, slot)

    @pl.when(j + 1 < n_tiles)
    def _():
        gather(rows_next_ref, 1 - slot)

    gcol = [jnp.transpose(jnp.broadcast_to(gate_ref[k:k + 1, :], (128, tm))) for k in range(2)]
    for k in range(2):
        pltpu.make_async_copy(ys_ref.at[pl.ds(0, tm)], ys_ref.at[pl.ds(0, tm)], sems.at[slot]).wait()
    gf = gf_ref[0]
    for cidx in range(x_ref.shape[1] // 128):
        cs = slice(cidx * 128, (cidx + 1) * 128)
        y0 = ybuf[slot, 0, :, :, cs].reshape(tm, 128)
        y1 = ybuf[slot, 1, :, :, cs].reshape(tm, 128)
        o_ref[:, cs] = x_ref[:, cs] + gf[:, cs] * (gcol[0] * y0 + gcol[1] * y1)


def _combine(rows, xf, gate, mod3, ys, *, tiles_per_row, ctx_row):
    t, d = xf.shape
    tm = COMBINE_TILE
    n_tiles = t // tm
    if ctx_row is None:
        mrow = lambda j: j // tiles_per_row
    else:
        mrow = lambda j: ctx_row
    rows4 = rows.reshape(2, t // SMEM_LANES, 1, SMEM_LANES)
    rows_block = (2, tm // SMEM_LANES, 1, SMEM_LANES)
    return pl.pallas_call(
        functools.partial(_combine_kernel, n_tiles=n_tiles),
        grid=(n_tiles,),
        in_specs=[pl.BlockSpec(rows_block, lambda j: (0, j, 0, 0), memory_space=pltpu.SMEM),
                  pl.BlockSpec(rows_block, lambda j: (0, jnp.minimum(j + 1, n_tiles - 1), 0, 0),
                               memory_space=pltpu.SMEM),
                  pl.BlockSpec((tm, d), lambda j: (j, 0)),
                  pl.BlockSpec((2, tm), lambda j: (0, j)),
                  pl.BlockSpec((1, 1, d), lambda j: (mrow(j), 0, 5)),
                  pl.BlockSpec(memory_space=pl.ANY)],
        out_specs=pl.BlockSpec((tm, d), lambda j: (j, 0)),
        scratch_shapes=[pltpu.VMEM((2, 2, tm // SUBLANES, SUBLANES, d), F32), pltpu.SemaphoreType.DMA((2,))],
        out_shape=jax.ShapeDtypeStruct((t, d), F32),
        compiler_params=_params(1),
        name="moe_combine",
    )(rows4, rows4, xf, gate, mod3, ys)


def _rope_tables(l):
    half = RET_HEAD_DIM // 4
    t = jnp.arange(l)
    rows = (t // GRID_W).astype(F32)
    cols = (t % GRID_W).astype(F32)
    freqs = ROPE_BASE ** (-jnp.arange(half, dtype=F32) / half)
    ang_r = rows[:, None] * freqs
    ang_c = cols[:, None] * freqs
    ang = jnp.concatenate([ang_r, ang_r, ang_c, ang_c], axis=1)
    sign = np.tile(np.concatenate([-np.ones(half), np.ones(half)]), 2).astype(np.float32)
    return jnp.cos(ang), jnp.sin(ang) * sign


def _slot_to_expert():
    r = np.arange(N_EXPERTS)
    return (r % N_GROUPS) * EXPERTS_PER_GROUP + r // N_GROUPS


def kernel(x, c, ctx, c_ctx, ada_w, ada_b, norm_mix_w, norm_ffn_w, w_in, na_q_norm, na_k_norm, na_rpb,
           ret_decay, w_out, router_w, router_bias, exp_w_gate, exp_w_up, exp_w_down):
    b, l, d = x.shape
    lc = ctx.shape[1]
    depth = ada_w.shape[0]
    assert b + 1 <= MOD_ROWS and l % (NA_Q_ROWS * GRID_W) == 0 and l // GRID_W >= NA_K_ROWS
    assert (b * l) % TOKEN_TILE == 0 and (b * lc) % TOKEN_TILE == 0 and l % TOKEN_TILE == 0

    cc = jnp.concatenate([c, c_ctx[None], jnp.zeros((MOD_ROWS - b - 1, d), F32)], axis=0)
    mods = _ada(cc, ada_w, ada_b)

    grp = jnp.asarray(np.kron(np.eye(NA_HEADS), np.full((NA_HEAD_DIM, NA_HEAD_DIM), 1.0 / NA_HEAD_DIM)), BF16)
    rope_tabs = _rope_tables(l)
    slot_exp = _slot_to_expert()
    rwt = router_w.T[slot_exp].astype(BF16)
    rb = router_bias[slot_exp].reshape(N_EXPERTS, 1).astype(F32)
    tri = jnp.asarray(np.triu(np.ones((TOKEN_TILE, TOKEN_TILE), np.float32), 1), BF16)
    bm = MOE_BLOCK_ROWS

    for layer in range(depth):
        last = layer == depth - 1
        mod3 = mods[layer].reshape(MOD_ROWS, 1, 6 * d)
        w_in_l = w_in[layer].astype(BF16)
        w_out_l = w_out[layer].astype(BF16)
        qn = jnp.tile(na_q_norm[layer], NA_HEADS)[None].astype(F32)
        kn = jnp.tile(na_k_norm[layer], NA_HEADS)[None].astype(F32)
        nw_mix = norm_mix_w[layer][None]
        nw_ffn = norm_ffn_w[layer][None]

        pc = _proj(ctx, mod3, nw_mix, w_in_l, grp, qn, kn, None, ctx_row=b)
        pq = _proj(x, mod3, nw_mix, w_in_l, grp, qn, kn, rope_tabs, ctx_row=None)
        qac, kac, vac, qbc, kbc, vbc, gfc, gbc = pc
        qa, ka, va, qb, kb, vb, gfw, gbw = pq

        zero_state = jnp.zeros((b, RET_HEADS, 2, RET_HEAD_DIM, RET_HEAD_DIM), F32)
        ret_c, ctx_state = _retention(ret_decay[layer], qbc, kbc, vbc, gfc, gbc, zero_state)
        ret_l, _ = _retention(ret_decay[layer], qb, kb, vb, gfw, gbw, ctx_state)
        na_l = _na_attention(qa, ka, va, kac, vac, _na_bias_table(na_rpb[layer]))

        cnt0 = jnp.zeros((N_EXPERTS, 1), F32)
        xf, tok_l, code_l, gate_l, cnt = _mix(
            x.reshape(b * l, d), na_l.reshape(b * l, NA_WIDTH), ret_l.reshape(b * l, RET_WIDTH), w_out_l, mod3,
            nw_ffn, rwt, rb, tri, cnt0, tiles_per_row=l // TOKEN_TILE, ctx_row=None)
        n_assign = 2 * b * l
        if not last:
            na_c = _ctx_attention(qac, kac, vac)
            cf, tok_c, code_c, gate_c, cnt = _mix(
                ctx.reshape(b * lc, d), na_c.reshape(b * lc, NA_WIDTH), ret_c.reshape(b * lc, RET_WIDTH), w_out_l,
                mod3, nw_ffn, rwt, rb, tri, cnt, tiles_per_row=1, ctx_row=b)
            n_assign += 2 * b * lc

        counts = cnt[:, 0].astype(I32)
        pcounts = (counts + bm - 1) // bm * bm
        pend = jnp.cumsum(pcounts)
        pstart = (pend - pcounts).astype(I32)
        n_rows = n_assign + N_EXPERTS * bm
        n_blocks = n_rows // bm
        nvalid = (pend[-1] // bm).astype(I32)
        blk = jnp.minimum(jnp.arange(n_blocks, dtype=I32), nvalid - 1) * bm
        blk_slot = jnp.minimum(jnp.sum(blk[:, None] >= pend[None, :], axis=1), N_EXPERTS - 1)
        blk_exp = jnp.asarray(slot_exp, I32)[blk_slot]

        pad_start = pstart + counts
        pad_len = pcounts - counts
        if last:
            rows_l = _sorted_rows(pstart, code_l)
            toks = [tok_l]
        else:
            rows_all = _sorted_rows(pstart, jnp.concatenate([code_l, code_c], axis=1))
            rows_l, rows_c = rows_all[:, :b * l], rows_all[:, b * l:]
            toks = [tok_l, tok_c]
        nvalid = nvalid.reshape(1)
        xs = _dispatch(pad_start, pad_len, nvalid, rows_l if last else rows_all, toks, n_rows=n_rows)
        ys = _ffn(blk_exp, nvalid, xs, exp_w_gate[layer], exp_w_up[layer], exp_w_down[layer])
        x = _combine(rows_l, xf, gate_l, mod3, ys, tiles_per_row=l // COMBINE_TILE, ctx_row=None).reshape(b, l, d)
        if not last:
            ctx = _combine(rows_c, cf, gate_c, mod3, ys, tiles_per_row=1, ctx_row=b).reshape(b, lc, d)
    return x
```
